```python
import jax
import jax.numpy as jnp
from jax import lax
import numpy as np

D_MODEL = 1024
BATCH = 4
SEQ = 4096
DEPTH = 4
DEC_BATCH = 32
DEC_SEQ = 1
PAST_LEN = 8192
PAGE_SIZE = 128

N_MIXERS = 4
N_HEADS = 16
HEAD_DIM = D_MODEL // N_HEADS
HD = N_HEADS * HEAD_DIM
D_FF = 2816
N_ADA = 9
FFN_RES_WEIGHT = 0.5
ROPE_THETA = 10000.0
LN_EPS = 1e-5
NEG_INF = -1e30
DN_ALPHA = (2 * DEPTH) ** 0.25
DN_BETA = (8 * DEPTH) ** -0.25

MOBA_BLOCK = 256
MOBA_TOPK = 3
MOBA_Q_CHUNK = 32

DSA_TOPK_MAX = 256
DSA_IDX_HEADS = 8
DSA_IDX_DIM = 64
DSA_Q_CHUNK = 64

FOX_Q_BLOCK = 128

DIL_WINDOWS = (128, 512, 2048)
DIL_DILATIONS = (1, 4, 16)
DIL_HEADS = 16
DIL_HD = DIL_HEADS * HEAD_DIM
DIL_BLOCK = 128

kernel_name = 'hybrid_moba_dsa_fox_dilated_decoder_step'


def layer_norm(x, g, b):
    xf = x.astype(jnp.float32)
    mu = jnp.mean(xf, -1, keepdims=True)
    var = jnp.mean(jnp.square(xf - mu), -1, keepdims=True)
    y = (xf - mu) * lax.rsqrt(var + LN_EPS) * g.astype(jnp.float32) + b.astype(jnp.float32)
    return y.astype(x.dtype)


def rope(x, pos):
    half = x.shape[-1] // 2
    inv = ROPE_THETA ** (-jnp.arange(half, dtype=jnp.float32) / half)
    ang = pos.astype(jnp.float32)[:, None] * inv
    cos, sin = jnp.cos(ang)[:, None, :], jnp.sin(ang)[:, None, :]
    xf = x.astype(jnp.float32)
    x1, x2 = xf[..., :half], xf[..., half:]
    return jnp.concatenate([x1 * cos - x2 * sin, x2 * cos + x1 * sin], -1).astype(x.dtype)


def swiglu(h, w_up, w_down):
    g, u = jnp.split(h @ w_up, 2, axis=-1)
    return (jax.nn.silu(g) * u) @ w_down


def modulate(x, ada, j):
    return x * (1 + ada[:, 3 * j + 1][:, None, :]) + ada[:, 3 * j][:, None, :]


def post_residual(x, y, ada, j, weight, g, b):
    return layer_norm(DN_ALPHA * x + weight * (1 + ada[:, 3 * j + 2][:, None, :]) * y, g, b)


def gather_pages(pool, page_table):
    g = pool[page_table]
    return g.reshape((g.shape[0], g.shape[1] * g.shape[2]) + pool.shape[2:])


def query_chunked(fn, chunk, pos, *qs):
    tq = pos.shape[0]
    c = min(chunk, tq)
    n = -(-tq // c)
    pad = n * c - tq
    pos_c = jnp.pad(pos, (0, pad)).reshape(n, c)

    def split(a):
        a = jnp.pad(a, [(0, 0), (0, pad)] + [(0, 0)] * (a.ndim - 2))
        return jnp.moveaxis(a.reshape((a.shape[0], n, c) + a.shape[2:]), 1, 0)

    out = lax.map(lambda args: fn(*args), (pos_c,) + tuple(split(a) for a in qs))
    out = jnp.moveaxis(out, 0, 1)
    return out.reshape((out.shape[0], n * c) + out.shape[3:])[:, :tq]


def moba_core(q, k, v, q_pos):
    B, S, H, dh = k.shape
    nb = -(-S // MOBA_BLOCK)
    pad = nb * MOBA_BLOCK - S
    kb = jnp.pad(k, ((0, 0), (0, pad), (0, 0), (0, 0))).reshape(B, nb, MOBA_BLOCK, H, dh)
    vb = jnp.pad(v, ((0, 0), (0, pad), (0, 0), (0, 0))).reshape(B, nb, MOBA_BLOCK, H, dh)
    kmean = jnp.mean(kb.astype(jnp.float32), axis=2)
    kb = kb.transpose(0, 3, 1, 2, 4)
    vb = vb.transpose(0, 3, 1, 2, 4)
    n_sel = min(MOBA_TOPK, nb)
    blk_ids = jnp.arange(nb, dtype=jnp.int32)
    in_blk = jnp.arange(MOBA_BLOCK, dtype=jnp.int32)
    is_own = (jnp.arange(n_sel + 1) == n_sel)[:, None]
    bi = jnp.arange(B)[:, None, None, None]
    hi = jnp.arange(H)[None, None, :, None]
    scale = dh ** -0.5

    def chunk(pb, qb):
        qc = pb.shape[0]
        own = pb // MOBA_BLOCK
        gate = jnp.einsum('bqhd,bnhd->bqhn', qb.astype(jnp.float32), kmean)
        gate = jnp.where((blk_ids[None, :] < own[:, None])[None, :, None, :], gate, NEG_INF)
        _, top = lax.top_k(gate, n_sel)
        idx = jnp.concatenate([top, jnp.broadcast_to(own[None, :, None, None], (B, qc, H, 1))], -1)
        ks = kb[bi, hi, idx]
        vs = vb[bi, hi, idx]
        s = jnp.einsum('bqhd,bqhnkd->bqhnk', qb, ks, preferred_element_type=jnp.float32) * scale
        kpos = idx[..., None] * MOBA_BLOCK + in_blk
        ok = jnp.where(is_own, kpos <= pb[None, :, None, None, None],
                       (idx < own[None, :, None, None])[..., None])
        s = jnp.where(ok, s, NEG_INF).reshape(B, qc, H, -1)
        p = jax.nn.softmax(s, axis=-1).astype(v.dtype)
        return jnp.einsum('bqhm,bqhmd->bqhd', p, vs.reshape(B, qc, H, -1, dh))

    return query_chunked(chunk, MOBA_Q_CHUNK, q_pos, q)


def mixer_moba(hp, hs, pos_p, pos_s, cache_kv, page_table, w_qkv, w_o):
    def proj(h, pos):
        b, t, _ = h.shape
        z = (h @ w_qkv).reshape(b, t, 3, N_HEADS, HEAD_DIM)
        return rope(z[:, :, 0], pos), rope(z[:, :, 1], pos), z[:, :, 2]

    qp, kp, vp = proj(hp, pos_p)
    op = moba_core(qp, kp, vp, pos_p)
    qs, ks, vs = proj(hs, pos_s)
    kv_past = gather_pages(cache_kv, page_table)
    k_all = jnp.concatenate([kv_past[:, :, 0], ks], 1)
    v_all = jnp.concatenate([kv_past[:, :, 1], vs], 1)
    os_ = moba_core(qs, k_all, v_all, pos_s)
    yp = op.reshape(op.shape[0], op.shape[1], HD) @ w_o
    ys = os_.reshape(os_.shape[0], os_.shape[1], HD) @ w_o
    return yp, ys, jnp.stack([kp, vp], 2), jnp.stack([ks, vs], 2)


def dsa_core(q, qi, wi, k, v, ki, q_pos, n_keep):
    B, S, H, dh = k.shape
    kpos = jnp.arange(S, dtype=jnp.int32)
    kif = ki.astype(jnp.float32)
    bi = jnp.arange(B)[:, None, None]
    scale = dh ** -0.5

    def chunk(pb, qb, qib, wb):
        rel = jax.nn.relu(jnp.einsum('bqhd,bsd->bqhs', qib.astype(jnp.float32), kif))
        score = jnp.einsum('bqh,bqhs->bqs', wb.astype(jnp.float32), rel)
        score = jnp.where((kpos[None, :] <= pb[:, None])[None], score, NEG_INF)
        _, idx = lax.top_k(score, n_keep)
        ks = k[bi, idx]
        vs = v[bi, idx]
        s = jnp.einsum('bqhd,bqkhd->bqhk', qb, ks, preferred_element_type=jnp.float32) * scale
        s = jnp.where((idx <= pb[None, :, None])[:, :, None, :], s, NEG_INF)
        p = jax.nn.softmax(s, axis=-1).astype(v.dtype)
        return jnp.einsum('bqhk,bqkhd->bqhd', p, vs)

    return query_chunked(chunk, DSA_Q_CHUNK, q_pos, q, qi, wi)


def mixer_dsa(hp, hs, pos_p, pos_s, cache_kv, cache_kidx, page_table, w_in, w_o):
    n_qi = DSA_IDX_HEADS * DSA_IDX_DIM
    w_scale = (DSA_IDX_HEADS ** -0.5) * (DSA_IDX_DIM ** -0.5)

    def proj(h, pos):
        b, t, _ = h.shape
        z = h @ w_in
        qkv = z[..., :3 * HD].reshape(b, t, 3, N_HEADS, HEAD_DIM)
        o = 3 * HD
        qi = z[..., o:o + n_qi].reshape(b, t, DSA_IDX_HEADS, DSA_IDX_DIM)
        ki = z[..., o + n_qi:o + n_qi + DSA_IDX_DIM]
        wi = z[..., o + n_qi + DSA_IDX_DIM:] * w_scale
        return (rope(qkv[:, :, 0], pos), rope(qkv[:, :, 1], pos), qkv[:, :, 2],
                rope(qi, pos), rope(ki[:, :, None, :], pos)[:, :, 0], wi)

    qp, kp, vp, qip, kip, wip = proj(hp, pos_p)
    op = dsa_core(qp, qip, wip, kp, vp, kip, pos_p, min(DSA_TOPK_MAX, SEQ // 4))
    qs, ks, vs, qis, kis, wis = proj(hs, pos_s)
    kv_past = gather_pages(cache_kv, page_table)
    ki_past = gather_pages(cache_kidx, page_table)
    k_all = jnp.concatenate([kv_past[:, :, 0], ks], 1)
    v_all = jnp.concatenate([kv_past[:, :, 1], vs], 1)
    ki_all = jnp.concatenate([ki_past, kis], 1)
    os_ = dsa_core(qs, qis, wis, k_all, v_all, ki_all, pos_s,
                   min(DSA_TOPK_MAX, (PAST_LEN + DEC_SEQ) // 4))
    yp = op.reshape(op.shape[0], op.shape[1], HD) @ w_o
    ys = os_.reshape(os_.shape[0], os_.shape[1], HD) @ w_o
    return yp, ys, jnp.stack([kp, vp], 2), jnp.stack([ks, vs], 2), kip, kis


def fox_core(q, k, v, fq, fk, q_pos):
    B, S, H, dh = k.shape
    kpos = jnp.arange(S, dtype=jnp.int32)
    fk_t = fk.transpose(0, 2, 1)
    scale = dh ** -0.5

    def chunk(pb, qb, fb):
        s = jnp.einsum('bqhd,bshd->bhqs', qb, k, preferred_element_type=jnp.float32) * scale
        s = s + fb.transpose(0, 2, 1)[..., None] - fk_t[:, :, None, :]
        s = jnp.where((kpos[None, :] <= pb[:, None])[None, None], s, NEG_INF)
        p = jax.nn.softmax(s, axis=-1).astype(v.dtype)
        return jnp.einsum('bhqs,bshd->bqhd', p, v)

    return query_chunked(chunk, FOX_Q_BLOCK, q_pos, q, fq)


def mixer_fox(hp, hs, pos_p, pos_s, cache_kv, cache_logf, page_table, w_in, b_f, w_o):
    def proj(h):
        b, t, _ = h.shape
        z = h @ w_in
        qkv = z[..., :3 * HD].reshape(b, t, 3, N_HEADS, HEAD_DIM)
        logf = jax.nn.log_sigmoid(z[..., 3 * HD:].astype(jnp.float32) + b_f.astype(jnp.float32))
        return qkv[:, :, 0], qkv[:, :, 1], qkv[:, :, 2], logf

    qp, kp, vp, lfp = proj(hp)
    fp = jnp.cumsum(lfp, axis=1)
    op = fox_core(qp, kp, vp, fp, fp, pos_p)
    qs, ks, vs, lfs = proj(hs)
    kv_past = gather_pages(cache_kv, page_table)
    lf_past = gather_pages(cache_logf, page_table).astype(jnp.float32)
    fk = jnp.cumsum(jnp.concatenate([lf_past, lfs], 1), axis=1)
    k_all = jnp.concatenate([kv_past[:, :, 0], ks], 1)
    v_all = jnp.concatenate([kv_past[:, :, 1], vs], 1)
    os_ = fox_core(qs, k_all, v_all, fk[:, PAST_LEN:], fk, pos_s)
    yp = op.reshape(op.shape[0], op.shape[1], HD) @ w_o
    ys = os_.reshape(os_.shape[0], os_.shape[1], HD) @ w_o
    return (yp, ys, jnp.stack([kp, vp], 2), jnp.stack([ks, vs], 2),
            lfp.astype(hp.dtype), lfs.astype(hs.dtype))


def dilated_band(q, k, v, dil, n_back):
    B, T, H, dh = q.shape
    ls = T // dil
    nb = -(-ls // DIL_BLOCK)
    pad = nb * DIL_BLOCK - ls

    def blocks(a):
        a = a.reshape(B, ls, dil, H, dh).transpose(0, 2, 1, 3, 4).reshape(B * dil, ls, H, dh)
        a = jnp.pad(a, ((0, 0), (0, pad), (0, 0), (0, 0)))
        return a.reshape(B * dil, nb, DIL_BLOCK, H, dh)

    def with_prev(a):
        prev = jnp.pad(a, ((0, 0), (1, 0), (0, 0), (0, 0), (0, 0)))[:, :-1]
        return jnp.concatenate([prev, a], 2)

    qb = blocks(q)
    kk, vv = with_prev(blocks(k)), with_prev(blocks(v))
    s = jnp.einsum('nbqhd,nbkhd->nbhqk', qb, kk, preferred_element_type=jnp.float32) * dh ** -0.5
    base = jnp.arange(nb, dtype=jnp.int32)[:, None, None] * DIL_BLOCK
    uq = base + jnp.arange(DIL_BLOCK, dtype=jnp.int32)[None, :, None]
    uk = base - DIL_BLOCK + jnp.arange(2 * DIL_BLOCK, dtype=jnp.int32)[None, None, :]
    ok = (uk >= 0) & (uq - uk >= 0) & (uq - uk <= n_back)
    s = jnp.where(ok[None, :, None], s, NEG_INF)
    lse = jax.nn.logsumexp(s, axis=-1)
    p = jnp.exp(s - lse[..., None]).astype(v.dtype)
    o = jnp.einsum('nbhqk,nbkhd->nbqhd', p, vv).reshape(B * dil, nb * DIL_BLOCK, H, dh)[:, :ls]
    o = o.reshape(B, dil, ls, H, dh).transpose(0, 2, 1, 3, 4).reshape(B, T, H, dh)
    lse = lse.transpose(0, 1, 3, 2).reshape(B * dil, nb * DIL_BLOCK, H)[:, :ls]
    lse = lse.reshape(B, dil, ls, H).transpose(0, 2, 1, 3).reshape(B, T, H)
    return o, lse


def dilated_gather(q, k_all, v_all, dil, n_back):
    tn = q.shape[1]
    la = k_all.shape[1]
    q_idx = la - tn + jnp.arange(tn, dtype=jnp.int32)
    idx = q_idx[:, None] - dil * jnp.arange(n_back + 1, dtype=jnp.int32)[None, :]
    ok = idx >= 0
    idx = jnp.maximum(idx, 0)
    ks, vs = k_all[:, idx], v_all[:, idx]
    s = jnp.einsum('bqhd,bqkhd->bhqk', q, ks, preferred_element_type=jnp.float32) * q.shape[-1] ** -0.5
    s = jnp.where(ok[None, None], s, NEG_INF)
    lse = jax.nn.logsumexp(s, axis=-1)
    p = jnp.exp(s - lse[..., None]).astype(v_all.dtype)
    return jnp.einsum('bhqk,bqkhd->bqhd', p, vs), lse.transpose(0, 2, 1)


def merge_groups(outs, lses):
    o = jnp.stack(outs, 0)
    w = jax.nn.softmax(jnp.stack(lses, 0), axis=0)
    y = jnp.einsum('gbthd,gbth->bthd', o, w.astype(o.dtype))
    return y.reshape(y.shape[0], y.shape[1], -1)


def mixer_dilated(hp, hs, pos_p, pos_s, buf_d1, buf_d4, buf_d16, w_qkv, w_o):
    n_grp = len(DIL_WINDOWS)

    def proj(h, pos):
        b, t, _ = h.shape
        z = (h @ w_qkv).reshape(b, t, n_grp, 3, DIL_HEADS, HEAD_DIM)
        return [(rope(z[:, :, g, 0], pos), rope(z[:, :, g, 1], pos), z[:, :, g, 2]) for g in range(n_grp)]

    grp_p, grp_s = proj(hp, pos_p), proj(hs, pos_s)
    bufs = (buf_d1, buf_d4, buf_d16)
    o_p, l_p, o_s, l_s, st_p, st_s = [], [], [], [], [], []
    for g in range(n_grp):
        win, dil = DIL_WINDOWS[g], DIL_DILATIONS[g]
        n_back = win // dil
        q, k, v = grp_p[g]
        o, l = dilated_band(q, k, v, dil, n_back)
        o_p.append(o)
        l_p.append(l)
        kv = jnp.stack([k, v], 2)
        st_p.append(kv[:, kv.shape[1] - min(win, SEQ):])
        q, k, v = grp_s[g]
        kv_all = jnp.concatenate([bufs[g], jnp.stack([k, v], 2)], 1)
        o, l = dilated_gather(q, kv_all[:, :, 0], kv_all[:, :, 1], dil, n_back)
        o_s.append(o)
        l_s.append(l)
        st_s.append(kv_all[:, kv_all.shape[1] - min(win, PAST_LEN + DEC_SEQ):])
    yp = merge_groups(o_p, l_p) @ w_o
    ys = merge_groups(o_s, l_s) @ w_o
    return yp, ys, st_p[0], st_p[1], st_p[2], st_s[0], st_s[1], st_s[2]


def _proj_weight(key, n_in, segments):
    w = jax.random.normal(key, (n_in, sum(s for s, _ in segments)), jnp.float32) * n_in ** -0.5
    col = jnp.concatenate([jnp.full((s,), c, jnp.float32) for s, c in segments])
    return w * col


def setup_inputs(seed: int = 0) -> dict:
    key = jax.random.key(seed)
    ks = jax.random.split(key, 28)
    f32 = jnp.float32
    n_pages = PAST_LEN // PAGE_SIZE
    n_used = DEC_BATCH * n_pages
    n_pool = n_used + (n_used + 3) // 4
    page_table = jax.random.permutation(ks[0], n_pool)[:n_used].reshape(DEC_BATCH, n_pages).astype(jnp.int32)
    kv_shape = (n_pool, PAGE_SIZE, 2, N_HEADS, HEAD_DIM)

    def nrm(k, shape, scale=1.0):
        return scale * jax.random.normal(k, shape, f32)

    def win_state(k, win):
        return nrm(k, (DEC_BATCH, min(win, PAST_LEN), 2, DIL_HEADS, HEAD_DIM))

    qkv_seg = ((HD, 1.0), (HD, 1.0), (HD, DN_BETA))
    dil_seg = ((DIL_HD, 1.0), (DIL_HD, 1.0), (DIL_HD, DN_BETA)) * len(DIL_WINDOWS)
    dsa_seg = qkv_seg + ((DSA_IDX_HEADS * DSA_IDX_DIM, 1.0), (DSA_IDX_DIM, 1.0), (DSA_IDX_HEADS, 1.0))
    return {
        'x_prompt': nrm(ks[1], (BATCH, SEQ, D_MODEL)),
        'x_sample': nrm(ks[2], (DEC_BATCH, DEC_SEQ, D_MODEL)),
        'c_prompt': nrm(ks[3], (BATCH, D_MODEL)),
        'c_sample': nrm(ks[4], (DEC_BATCH, D_MODEL)),
        'cache_kv_moba': nrm(ks[5], kv_shape),
        'cache_kv_dsa': nrm(ks[6], kv_shape),
        'cache_kidx_dsa': nrm(ks[7], (n_pool, PAGE_SIZE, DSA_IDX_DIM)),
        'cache_kv_fox': nrm(ks[8], kv_shape),
        'cache_logf_fox': jax.nn.log_sigmoid(2.5 + nrm(ks[9], (n_pool, PAGE_SIZE, N_HEADS))),
        'state_win_d1': win_state(ks[10], DIL_WINDOWS[0]),
        'state_win_d4': win_state(ks[11], DIL_WINDOWS[1]),
        'state_win_d16': win_state(ks[12], DIL_WINDOWS[2]),
        'page_table': page_table,
        'w_ada': nrm(ks[13], (DEPTH, D_MODEL, N_ADA * D_MODEL), 0.1 * D_MODEL ** -0.5),
        'b_ada': nrm(ks[14], (DEPTH, N_ADA * D_MODEL), 0.02),
        'ln_g': 1.0 + nrm(ks[15], (DEPTH, 3, D_MODEL), 0.02),
        'ln_b': nrm(ks[16], (DEPTH, 3, D_MODEL), 0.02),
        'ffn_w_up': nrm(ks[17], (DEPTH, 2, D_MODEL, 2 * D_FF), D_MODEL ** -0.5),
        'ffn_w_down': nrm(ks[18], (DEPTH, 2, D_FF, D_MODEL), DN_BETA * D_FF ** -0.5),
        'moba_w_qkv': _proj_weight(ks[19], D_MODEL, qkv_seg),
        'moba_w_o': nrm(ks[20], (HD, D_MODEL), DN_BETA * HD ** -0.5),
        'dsa_w_in': _proj_weight(ks[21], D_MODEL, dsa_seg),
        'dsa_w_o': nrm(ks[22], (HD, D_MODEL), DN_BETA * HD ** -0.5),
        'fox_w_in': _proj_weight(ks[23], D_MODEL, qkv_seg + ((N_HEADS, 1.0),)),
        'fox_b_f': jax.random.uniform(ks[24], (N_HEADS,), f32, 1.0, 4.0),
        'fox_w_o': nrm(ks[25], (HD, D_MODEL), DN_BETA * HD ** -0.5),
        'dil_w_qkv': _proj_weight(ks[26], D_MODEL, dil_seg),
        'dil_w_o': nrm(ks[27], (DIL_HD, D_MODEL), DN_BETA * DIL_HD ** -0.5),
    }


def reference(x_prompt, x_sample, c_prompt, c_sample, cache_kv_moba, cache_kv_dsa, cache_kidx_dsa,
              cache_kv_fox, cache_logf_fox, state_win_d1, state_win_d4, state_win_d16, page_table,
              w_ada, b_ada, ln_g, ln_b, ffn_w_up, ffn_w_down, moba_w_qkv, moba_w_o, dsa_w_in, dsa_w_o,
              fox_w_in, fox_b_f, fox_w_o, dil_w_qkv, dil_w_o):
    pos_p = jnp.arange(SEQ, dtype=jnp.int32)
    pos_s = PAST_LEN + jnp.arange(DEC_SEQ, dtype=jnp.int32)
    xp, xs = x_prompt, x_sample
    for i in range(DEPTH):
        ada_p = (c_prompt @ w_ada[i] + b_ada[i]).reshape(c_prompt.shape[0], N_ADA, D_MODEL)
        ada_s = (c_sample @ w_ada[i] + b_ada[i]).reshape(c_sample.shape[0], N_ADA, D_MODEL)
        xp = post_residual(xp, swiglu(modulate(xp, ada_p, 0), ffn_w_up[i, 0], ffn_w_down[i, 0]),
                           ada_p, 0, FFN_RES_WEIGHT, ln_g[i, 0], ln_b[i, 0])
        xs = post_residual(xs, swiglu(modulate(xs, ada_s, 0), ffn_w_up[i, 0], ffn_w_down[i, 0]),
                           ada_s, 0, FFN_RES_WEIGHT, ln_g[i, 0], ln_b[i, 0])
        hp, hs = modulate(xp, ada_p, 1), modulate(xs, ada_s, 1)
        kind = i % N_MIXERS
        if kind == 0:
            mp, ms, kv_moba_p, kv_moba_s = mixer_moba(hp, hs, pos_p, pos_s, cache_kv_moba, page_table,
                                                      moba_w_qkv, moba_w_o)
        elif kind == 1:
            mp, ms, kv_dsa_p, kv_dsa_s, kidx_dsa_p, kidx_dsa_s = mixer_dsa(
                hp, hs, pos_p, pos_s, cache_kv_dsa, cache_kidx_dsa, page_table, dsa_w_in, dsa_w_o)
        elif kind == 2:
            mp, ms, kv_fox_p, kv_fox_s, logf_fox_p, logf_fox_s = mixer_fox(
                hp, hs, pos_p, pos_s, cache_kv_fox, cache_logf_fox, page_table, fox_w_in, fox_b_f, fox_w_o)
        else:
            mp, ms, win_d1_p, win_d4_p, win_d16_p, win_d1_s, win_d4_s, win_d16_s = mixer_dilated(
                hp, hs, pos_p, pos_s, state_win_d1, state_win_d4, state_win_d16, dil_w_qkv, dil_w_o)
        xp = post_residual(xp, mp, ada_p, 1, 1.0, ln_g[i, 1], ln_b[i, 1])
        xs = post_residual(xs, ms, ada_s, 1, 1.0, ln_g[i, 1], ln_b[i, 1])
        xp = post_residual(xp, swiglu(modulate(xp, ada_p, 2), ffn_w_up[i, 1], ffn_w_down[i, 1]),
                           ada_p, 2, FFN_RES_WEIGHT, ln_g[i, 2], ln_b[i, 2])
        xs = post_residual(xs, swiglu(modulate(xs, ada_s, 2), ffn_w_up[i, 1], ffn_w_down[i, 1]),
                           ada_s, 2, FFN_RES_WEIGHT, ln_g[i, 2], ln_b[i, 2])
    return (xp, xs, kv_moba_p, kv_moba_s, kv_dsa_p, kv_dsa_s, kidx_dsa_p, kidx_dsa_s,
            kv_fox_p, kv_fox_s, logf_fox_p, logf_fox_s, win_d1_p, win_d1_s, win_d4_p, win_d4_s,
            win_d16_p, win_d16_s)
```

```python
import functools

import jax
import jax.numpy as jnp
from jax import lax
from jax.experimental import pallas as pl
from jax.experimental.pallas import tpu as pltpu

F32 = jnp.float32
BF16 = jnp.bfloat16
I32 = jnp.int32
HIGHEST = lax.Precision.HIGHEST

D_MODEL = 1024
N_HEADS = 16
HEAD_DIM = 64
HD = N_HEADS * HEAD_DIM
D_FF = 2816
N_ADA = 9
DEPTH = 4
PAST_LEN = 8192
PAGE_SIZE = 128
FFN_RES_WEIGHT = 0.5
ROPE_THETA = 10000.0
LN_EPS = 1e-5
NEG_INF = -1e30
DN_ALPHA = (2 * DEPTH) ** 0.25
MOBA_BLOCK = 256
MOBA_TOPK = 3
DSA_TOPK_MAX = 256
DSA_IDX_HEADS = 8
DSA_IDX_DIM = 64
DSA_W_SCALE = (DSA_IDX_HEADS ** -0.5) * (DSA_IDX_DIM ** -0.5)
DIL_WINDOWS = (128, 512, 2048)
DIL_DILATIONS = (1, 4, 16)
DIL_BLOCK = 128
QK_SCALE = HEAD_DIM ** -0.5

LANES = 128
VMEM_LIMIT = 56 * 2 ** 20
INT_MIN = -2 ** 31
NEG_KEY = -1900671691

_NT = (((1,), (1,)), ((), ()))


def _cp(*sem):
    return pltpu.CompilerParams(dimension_semantics=sem, vmem_limit_bytes=VMEM_LIMIT)


def _dot(a, b, **kw):
    return jnp.dot(a, b, preferred_element_type=F32, **kw)


def _dot_nt(a, b, **kw):
    return lax.dot_general(a, b, _NT, preferred_element_type=F32, **kw)


def _iota(shape, dim):
    return lax.broadcasted_iota(I32, shape, dim)


def _sort_key(x):
    x = jnp.where(x == 0.0, 0.0, x)
    bits = pltpu.bitcast(x, I32)
    return jnp.where(bits < 0, bits ^ 0x7FFFFFFF, bits)


def _post_ln(x, y, gate, g, b, weight):
    r = DN_ALPHA * x + (weight * (1.0 + gate)) * y
    mu = jnp.mean(r, axis=-1, keepdims=True)
    rc = r - mu
    var = jnp.mean(rc * rc, axis=-1, keepdims=True)
    return rc * lax.rsqrt(var + LN_EPS) * g + b


def _ada_operand(ada, j, k, per_row):
    if per_row:
        m = ada.shape[0]
        return ada, pl.BlockSpec((m, D_MODEL), lambda *g: (0, 3 * j + k))
    return ada, pl.BlockSpec((None, None, 1, D_MODEL), lambda *g: (g[0], 3 * j + k, 0, 0))


def _ada_body(c_ref, w_ref, b_ref, o_ref):
    o_ref[...] = _dot(c_ref[...].astype(BF16), w_ref[...].astype(BF16)) + b_ref[...]


def _ada_all(c_all, w_ada, b_ada):
    m = c_all.shape[0]
    depth, d, n = w_ada.shape
    tn = 1152
    return pl.pallas_call(
        _ada_body,
        out_shape=jax.ShapeDtypeStruct((depth, m, n), F32),
        grid=(depth, n // tn),
        in_specs=[pl.BlockSpec((m, d), lambda l, c: (0, 0)),
                  pl.BlockSpec((None, d, tn), lambda l, c: (l, 0, c)),
                  pl.BlockSpec((None, 1, tn), lambda l, c: (l, 0, c))],
        out_specs=pl.BlockSpec((None, m, tn), lambda l, c: (l, 0, c)),
        compiler_params=_cp("arbitrary", "arbitrary"),
        name="ada",
    )(c_all, w_ada, b_ada.reshape(depth, 1, n))


def _ffn_body(x_ref, sh_ref, sc_ref, gt_ref, wg_ref, wu_ref, wd_ref, lg_ref, lb_ref, o_ref,
              h_scr, acc_scr):
    f = pl.program_id(2)

    @pl.when(f == 0)
    def _():
        h_scr[...] = (x_ref[...] * (1.0 + sc_ref[...]) + sh_ref[...]).astype(BF16)

    h = h_scr[...]
    g = _dot(h, wg_ref[...])
    u = _dot(h, wu_ref[...])
    a = (g * jax.nn.sigmoid(g) * u).astype(BF16)
    y = _dot(a, wd_ref[...])

    @pl.when(f == 0)
    def _():
        acc_scr[...] = y

    @pl.when(f > 0)
    def _():
        acc_scr[...] += y

    @pl.when(f == pl.num_programs(2) - 1)
    def _():
        o_ref[...] = _post_ln(x_ref[...], acc_scr[...], gt_ref[...], lg_ref[...], lb_ref[...],
                              FFN_RES_WEIGHT)


def _ffn(x, ada, j, w_up, w_down, ln_g, ln_b, per_row):
    b, t, d = x.shape
    ff = w_down.shape[0]
    tm = min(t, 512)
    fc = 1408 if ff % 1408 == 0 else ff
    nf = ff // fc
    sh, sh_spec = _ada_operand(ada, j, 0, per_row)
    sc, sc_spec = _ada_operand(ada, j, 1, per_row)
    gt, gt_spec = _ada_operand(ada, j, 2, per_row)
    return pl.pallas_call(
        _ffn_body,
        out_shape=jax.ShapeDtypeStruct((b, t, d), F32),
        grid=(b, t // tm, nf),
        in_specs=[pl.BlockSpec((None, tm, d), lambda bi, i, f: (bi, i, 0)),
                  sh_spec, sc_spec, gt_spec,
                  pl.BlockSpec((d, fc), lambda bi, i, f: (0, f)),
                  pl.BlockSpec((d, fc), lambda bi, i, f: (0, f + nf)),
                  pl.BlockSpec((fc, d), lambda bi, i, f: (f, 0)),
                  pl.BlockSpec((1, d), lambda bi, i, f: (0, 0)),
                  pl.BlockSpec((1, d), lambda bi, i, f: (0, 0))],
        out_specs=pl.BlockSpec((None, tm, d), lambda bi, i, f: (bi, i, 0)),
        scratch_shapes=[pltpu.VMEM((tm, d), BF16), pltpu.VMEM((tm, d), F32)],
        compiler_params=_cp("arbitrary", "arbitrary", "arbitrary"),
        name="ffn",
    )(x, sh, sc, gt, w_up, w_up, w_down, ln_g.reshape(1, d), ln_b.reshape(1, d))


def _proj_body(*refs, rope, extra):
    x_ref, sh_ref, sc_ref, cos_ref, sin_ref, w_ref = refs[:6]
    rest = refs[6:]
    if extra == "fox":
        bf_ref, q_ref, kv_ref, lf_ref, fc_ref, carry_scr = rest
    elif extra == "dsa":
        q_ref, kv_ref, qi_ref, kiwi_ref = rest
    else:
        q_ref, kv_ref = rest
    tm = x_ref.shape[0]
    h = (x_ref[...] * (1.0 + sc_ref[...]) + sh_ref[...]).astype(BF16)
    cos = cos_ref[...]
    sin = sin_ref[...]
    lane = _iota((tm, LANES), 1)
    first_half = (lane & 32) == 0

    def rope128(zc):
        sw = jnp.where(first_half, pltpu.roll(zc, 96, 1), pltpu.roll(zc, 32, 1))
        return zc * cos + sw * sin

    zq = _dot(h, w_ref[:, 0:HD])
    for s in range(HD // LANES):
        sl = slice(s * LANES, (s + 1) * LANES)
        c = zq[:, sl]
        q_ref[:, sl] = ((rope128(c) if rope else c) * QK_SCALE).astype(BF16)
    zk = _dot(h, w_ref[:, HD:2 * HD])
    for s in range(HD // LANES):
        sl = slice(s * LANES, (s + 1) * LANES)
        c = zk[:, sl]
        kv_ref[:, sl] = rope128(c) if rope else c
    kv_ref[:, HD:2 * HD] = _dot(h, w_ref[:, 2 * HD:3 * HD])

    if extra == "dsa":
        nqi = DSA_IDX_HEADS * DSA_IDX_DIM
        ze = _dot(h, w_ref[:, 3 * HD:3 * HD + nqi + LANES])
        for s in range(nqi // LANES):
            sl = slice(s * LANES, (s + 1) * LANES)
            qi_ref[:, sl] = rope128(ze[:, sl]).astype(BF16)
        c = ze[:, nqi:nqi + LANES]
        kiwi_ref[...] = jnp.where(lane < DSA_IDX_DIM, rope128(c), c * DSA_W_SCALE)
    elif extra == "fox":
        zf = _dot(h, w_ref[:, 3 * HD:3 * HD + LANES]) + bf_ref[...]
        lf = jnp.minimum(zf, 0.0) - jnp.log1p(jnp.exp(-jnp.abs(zf)))
        lf_ref[...] = lf

        @pl.when(pl.program_id(1) == 0)
        def _():
            carry_scr[...] = jnp.zeros_like(carry_scr)

        tri = (_iota((tm, tm), 0) >= _iota((tm, tm), 1)).astype(F32)
        fc = _dot(tri, lf, precision=HIGHEST) + carry_scr[...]
        fc_ref[...] = fc
        carry_scr[...] = fc[tm - 1:tm, :]


def _proj(x, ada, w, cos, sin, per_row, rope, extra=None, b_f=None):
    b, t, d = x.shape
    n = w.shape[1]
    tm = min(t, 512)
    sh, sh_spec = _ada_operand(ada, 1, 0, per_row)
    sc, sc_spec = _ada_operand(ada, 1, 1, per_row)
    tcs = cos.shape[0]
    cs_spec = (pl.BlockSpec((1, LANES), lambda bi, i: (0, 0)) if tcs == 1
               else pl.BlockSpec((tm, LANES), lambda bi, i: (i, 0)))
    row = lambda w_: pl.BlockSpec((None, tm, w_), lambda bi, i: (bi, i, 0))
    in_specs = [row(d), sh_spec, sc_spec, cs_spec, cs_spec,
                pl.BlockSpec((d, n), lambda bi, i: (0, 0))]
    args = [x, sh, sc, cos, sin, w]
    out_shape = [jax.ShapeDtypeStruct((b, t, HD), BF16), jax.ShapeDtypeStruct((b, t, 2 * HD), F32)]
    out_specs = [row(HD), row(2 * HD)]
    scratch = []
    if extra == "dsa":
        nqi = DSA_IDX_HEADS * DSA_IDX_DIM
        out_shape += [jax.ShapeDtypeStruct((b, t, nqi), BF16), jax.ShapeDtypeStruct((b, t, LANES), F32)]
        out_specs += [row(nqi), row(LANES)]
    elif extra == "fox":
        in_specs.append(pl.BlockSpec((1, LANES), lambda bi, i: (0, 0)))
        args.append(b_f)
        out_shape += [jax.ShapeDtypeStruct((b, t, LANES), F32), jax.ShapeDtypeStruct((b, t, LANES), F32)]
        out_specs += [row(LANES), row(LANES)]
        scratch = [pltpu.VMEM((1, LANES), F32)]
    return pl.pallas_call(
        functools.partial(_proj_body, rope=rope, extra=extra),
        out_shape=out_shape,
        grid=(b, t // tm),
        in_specs=in_specs,
        out_specs=out_specs,
        scratch_shapes=scratch,
        compiler_params=_cp("arbitrary", "arbitrary"),
        name="proj_" + (extra or "qkv"),
    )(*args)


def _oproj_body(*refs, n_grp):
    x_ref, gt_ref = refs[:2]
    o_refs = refs[2:2 + n_grp]
    l_refs = refs[2 + n_grp:2 + 2 * n_grp] if n_grp > 1 else ()
    w_ref, lg_ref, lb_ref, out_ref = refs[2 + len(o_refs) + len(l_refs):]
    if n_grp == 1:
        a = o_refs[0][...]
    else:
        ls = [r[...] for r in l_refs]
        m = functools.reduce(jnp.maximum, ls)
        es = [jnp.exp(l - m) for l in ls]
        den = functools.reduce(lambda p, q: p + q, es)
        a = functools.reduce(lambda p, q: p + q, [(e / den) * r[...] for e, r in zip(es, o_refs)])
        a = a.astype(BF16)
    y = _dot(a, w_ref[...])
    out_ref[...] = _post_ln(x_ref[...], y, gt_ref[...], lg_ref[...], lb_ref[...], 1.0)


def _oproj(x, ada, outs, lses, w_o, ln_g, ln_b, per_row):
    b, t, d = x.shape
    tm = min(t, 512)
    gt, gt_spec = _ada_operand(ada, 1, 2, per_row)
    row = pl.BlockSpec((None, tm, d), lambda bi, i: (bi, i, 0))
    one = pl.BlockSpec((1, d), lambda bi, i: (0, 0))
    n_grp = len(outs)
    return pl.pallas_call(
        functools.partial(_oproj_body, n_grp=n_grp),
        out_shape=jax.ShapeDtypeStruct((b, t, d), F32),
        grid=(b, t // tm),
        in_specs=[row, gt_spec] + [row] * (n_grp + len(lses))
        + [pl.BlockSpec((HD, d), lambda bi, i: (0, 0)), one, one],
        out_specs=row,
        compiler_params=_cp("arbitrary", "arbitrary"),
        name="oproj",
    )(x, gt, *outs, *lses, w_o, ln_g.reshape(1, d), ln_b.reshape(1, d))


TQ = 256


def _top3_lanes(g, blk):
    idx = []
    for _ in range(MOBA_TOPK):
        m = jnp.max(g, axis=1, keepdims=True)
        ix = jnp.min(jnp.where(g == m, blk, 1e9), axis=1, keepdims=True)
        idx.append(ix)
        g = jnp.where(blk == ix, -3e38, g)
    return idx


def _flash_body(*refs, mode):
    if mode == "moba":
        q_ref, k_ref, v_ref, o_ref, kmean_scr = refs
    elif mode == "dsa":
        q_ref, k_ref, v_ref, mask_ref, o_ref = refs
    else:
        q_ref, k_ref, v_ref, f_ref, ft_ref, o_ref = refs
    hp = pl.program_id(1)
    j = pl.program_id(2)
    lane = _iota((TQ, LANES), 1)
    lo = lane < HEAD_DIM
    q = q_ref[...].astype(F32)
    qh = (jnp.where(lo, q, 0.0).astype(BF16), jnp.where(lo, 0.0, q).astype(BF16))
    causal = _iota((TQ, TQ), 0) >= _iota((TQ, TQ), 1)

    if mode == "moba":
        nb = k_ref.shape[0] // MOBA_BLOCK
        nbp = kmean_scr.shape[0]

        @pl.when(j == 0)
        def _():
            kmean_scr[...] = jnp.zeros_like(kmean_scr)
            for n in range(nb):
                blk_k = k_ref[n * MOBA_BLOCK:(n + 1) * MOBA_BLOCK, :]
                kmean_scr[n:n + 1, :] = jnp.sum(blk_k, axis=0, keepdims=True) * (1.0 / MOBA_BLOCK)

        km = kmean_scr[...]
        blk = _iota((TQ, nbp), 1).astype(F32)
        jf = j.astype(F32)
        sel_idx = []
        for h in range(2):
            g = _dot_nt(jnp.where(lo, q, 0.0) if h == 0 else jnp.where(lo, 0.0, q), km, precision=HIGHEST)
            sel_idx.append(_top3_lanes(jnp.where(blk < jf, g, NEG_INF), blk))
    elif mode == "fox":
        f_tile = f_ref[...]
        fq = [jnp.sum(jnp.where(lane == 2 * hp + h, f_tile, 0.0), axis=1, keepdims=True)
              for h in range(2)]

    def step(n, carry, diag):
        m_old, l_old, acc = carry
        start = pl.multiple_of(n * TQ, TQ)
        kt = k_ref[pl.ds(start, TQ), :].astype(BF16)
        vt = v_ref[pl.ds(start, TQ), :].astype(BF16)
        m_new, l_new, alphas, pvs = [], [], [], []
        for h in range(2):
            s = _dot_nt(qh[h], kt)
            if mode == "fox":
                s = (s + fq[h]) - ft_ref[pl.ds(2 * hp + h, 1), pl.ds(start, TQ)]
            elif mode == "dsa":
                s = jnp.where(mask_ref[:, pl.ds(start, TQ)].astype(F32) > 0.5, s, NEG_INF)
            elif not diag:
                nf = n.astype(F32)
                i1, i2, i3 = sel_idx[h]
                s = jnp.where((i1 == nf) | (i2 == nf) | (i3 == nf), s, NEG_INF)
            if diag:
                s = jnp.where(causal, s, NEG_INF)
            mn = jnp.maximum(m_old[h], jnp.max(s, axis=1, keepdims=True))
            alpha = jnp.exp(m_old[h] - mn)
            p = jnp.exp(s - mn)
            m_new.append(mn)
            l_new.append(alpha * l_old[h] + jnp.sum(p, axis=1, keepdims=True))
            alphas.append(alpha)
            pvs.append(_dot(p.astype(BF16), vt))
        acc = jnp.where(lo, alphas[0], alphas[1]) * acc + jnp.where(lo, pvs[0], pvs[1])
        return tuple(m_new), tuple(l_new), acc

    col0 = jnp.zeros((TQ, 1), F32)
    init = ((col0 - jnp.inf, col0 - jnp.inf), (col0, col0), jnp.zeros((TQ, LANES), F32))
    carry = step(j, init, True)
    _, l_fin, acc = lax.fori_loop(0, j, lambda n, c: step(n, c, False), carry)
    o_ref[...] = (acc / jnp.where(lo, l_fin[0], l_fin[1])).astype(BF16)


def _flash(q, kv, mode, extra=()):
    b, t, _ = q.shape
    nq = t // TQ
    nhp = HD // LANES
    qspec = pl.BlockSpec((None, TQ, LANES), lambda bi, hp, j: (bi, j, hp))
    in_specs = [qspec,
                pl.BlockSpec((None, t, LANES), lambda bi, hp, j: (bi, 0, hp)),
                pl.BlockSpec((None, t, LANES), lambda bi, hp, j: (bi, 0, nhp + hp))]
    scratch = []
    if mode == "moba":
        nbp = max(LANES, -(-nq // LANES) * LANES)
        scratch = [pltpu.VMEM((nbp, LANES), F32)]
    elif mode == "dsa":
        in_specs.append(pl.BlockSpec((None, TQ, t), lambda bi, hp, j: (bi, j, 0)))
    else:
        in_specs += [pl.BlockSpec((None, TQ, LANES), lambda bi, hp, j: (bi, j, 0)),
                     pl.BlockSpec((None, N_HEADS, t), lambda bi, hp, j: (bi, 0, 0))]
    return pl.pallas_call(
        functools.partial(_flash_body, mode=mode),
        out_shape=jax.ShapeDtypeStruct((b, t, HD), BF16),
        grid=(b, nhp, nq),
        in_specs=in_specs,
        out_specs=qspec,
        scratch_shapes=scratch,
        compiler_params=_cp("arbitrary", "arbitrary", "arbitrary"),
        name="flash_" + mode,
    )(q, kv, kv, *extra)


TQI = 128
ICH = 512


def _count(key_scr, n_chunks, pred):
    rows = key_scr.shape[0]

    def body(c, cnt):
        st = pl.multiple_of(c * ICH, ICH)
        x = jnp.where(pred(key_scr[:, pl.ds(st, ICH)], st), 1.0, 0.0)
        for s in range(ICH // LANES):
            cnt = cnt + x[:, s * LANES:(s + 1) * LANES]
        return cnt

    cnt = lax.fori_loop(0, n_chunks, body, jnp.zeros((rows, LANES), F32))
    return jnp.sum(cnt, axis=1, keepdims=True)


def _kth_largest_key(key_scr, n_chunks, k, extra_key=None):
    rows = key_scr.shape[0]

    def count_ge(cand):
        c = _count(key_scr, n_chunks, lambda kc, st: kc >= cand)
        if extra_key is not None:
            c = c + jnp.where(extra_key >= cand, 1.0, 0.0)
        return c

    t0 = jnp.where(count_ge(jnp.zeros((rows, 1), I32)) >= k, 0, INT_MIN).astype(I32)

    def bit_body(it, cur):
        cand = cur + lax.shift_left(jnp.int32(1), 30 - it)
        return jnp.where(count_ge(cand) >= k, cand, cur)

    return lax.fori_loop(0, 31, bit_body, t0)


def _tie_limit(key_scr, n_chunks, thr, need, n_bits, lim_scr):
    rows = key_scr.shape[0]

    def bit_body(it, cur):
        cand = cur + lax.shift_left(jnp.int32(1), n_bits - 1 - it)
        c = _count(key_scr, n_chunks,
                   lambda kc, st: (kc == thr) & (st + _iota(kc.shape, 1) < cand))
        return jnp.where(c < need, cand, cur)

    lim = lax.fori_loop(0, n_bits, bit_body, jnp.zeros((rows, 1), I32))
    lim_scr[...] = jnp.broadcast_to(lim, lim_scr.shape)


def _dsa_index_body(qi_ref, wq_ref, kk_ref, mask_ref, key_scr, lim_scr, *, n_keep):
    i = pl.program_id(1)
    t = kk_ref.shape[0]
    nch = t // ICH
    qpos = i * TQI + _iota((TQI, 1), 0)
    nact = ((i + 1) * TQI + ICH - 1) // ICH
    wq = wq_ref[...]
    qi = qi_ref[...]

    for c in range(nch):
        @pl.when(c < nact)
        def _(c=c):
            kc = kk_ref[c * ICH:(c + 1) * ICH, :][:, :DSA_IDX_DIM].astype(BF16)
            acc = jnp.zeros((TQI, ICH), F32)
            for h in range(DSA_IDX_HEADS):
                d = _dot_nt(qi[:, h * DSA_IDX_DIM:(h + 1) * DSA_IDX_DIM], kc)
                acc = acc + wq[:, DSA_IDX_DIM + h:DSA_IDX_DIM + h + 1] * jnp.maximum(d, 0.0)
            kpos = c * ICH + _iota((TQI, ICH), 1)
            key_scr[:, c * ICH:(c + 1) * ICH] = _sort_key(jnp.where(kpos <= qpos, acc, NEG_INF))

    kf = float(n_keep)
    thr = _kth_largest_key(key_scr, nact, kf)
    n_gt = _count(key_scr, nact, lambda kc, st: kc > thr)
    n_eq = _count(key_scr, nact, lambda kc, st: kc == thr)
    need = kf - n_gt
    lim_scr[...] = jnp.full(lim_scr.shape, t, I32)
    has_tie = jnp.max(jnp.where((n_eq > need) & (thr > NEG_KEY), 1.0, 0.0)) > 0.5

    @pl.when(has_tie)
    def _():
        _tie_limit(key_scr, nact, thr, need, max(1, (t - 1).bit_length()), lim_scr)

    lim = lim_scr[:, 0:1]
    for c in range(nch):
        @pl.when(c < nact)
        def _(c=c):
            kc = key_scr[:, c * ICH:(c + 1) * ICH]
            kpos = c * ICH + _iota((TQI, ICH), 1)
            tie_ok = jnp.where(kc == thr, jnp.where(kpos <= lim, 1, 0), 0)
            sel = jnp.where(kc > thr, 1, tie_ok)
            mask_ref[:, c * ICH:(c + 1) * ICH] = jnp.where(kpos <= qpos, sel, 0).astype(jnp.int8)

        @pl.when(c >= nact)
        def _(c=c):
            mask_ref[:, c * ICH:(c + 1) * ICH] = jnp.zeros((TQI, ICH), jnp.int8)


def _dsa_index(qi, kiwi, n_keep):
    b, t, _ = qi.shape
    return pl.pallas_call(
        functools.partial(_dsa_index_body, n_keep=n_keep),
        out_shape=jax.ShapeDtypeStruct((b, t, t), jnp.int8),
        grid=(b, t // TQI),
        in_specs=[pl.BlockSpec((None, TQI, qi.shape[2]), lambda bi, i: (bi, i, 0)),
                  pl.BlockSpec((None, TQI, LANES), lambda bi, i: (bi, i, 0)),
                  pl.BlockSpec((None, t, LANES), lambda bi, i: (bi, 0, 0))],
        out_specs=pl.BlockSpec((None, TQI, t), lambda bi, i: (bi, i, 0)),
        scratch_shapes=[pltpu.VMEM((TQI, t), I32), pltpu.VMEM((TQI, LANES), I32)],
        compiler_params=_cp("arbitrary", "arbitrary"),
        name="dsa_index",
    )(qi, kiwi, kiwi)


def _dil_body(q_ref, kp_ref, kc_ref, vp_ref, vc_ref, o_ref, lse_ref):
    i = pl.program_id(2)
    n = DIL_BLOCK
    lane = _iota((n, LANES), 1)
    lo = lane < HEAD_DIM
    row = _iota((n, n), 0)
    col = _iota((n, n), 1)
    ok_cur = row >= col
    ok_prev = (col >= row) & (i > 0)
    for hp in range(HD // LANES):
        sl = slice(hp * LANES, (hp + 1) * LANES)
        q = q_ref[:, sl].astype(F32)
        kc = kc_ref[:, sl].astype(BF16)
        kp = kp_ref[:, sl].astype(BF16)
        vc = vc_ref[:, sl].astype(BF16)
        vp = vp_ref[:, sl].astype(BF16)
        outs, lses = [], []
        for h in range(2):
            qh = (jnp.where(lo, q, 0.0) if h == 0 else jnp.where(lo, 0.0, q)).astype(BF16)
            sc = jnp.where(ok_cur, _dot_nt(qh, kc), NEG_INF)
            sp = jnp.where(ok_prev, _dot_nt(qh, kp), NEG_INF)
            m = jnp.maximum(jnp.max(sc, axis=1, keepdims=True), jnp.max(sp, axis=1, keepdims=True))
            pc = jnp.exp(sc - m)
            pp = jnp.exp(sp - m)
            l = jnp.sum(pc, axis=1, keepdims=True) + jnp.sum(pp, axis=1, keepdims=True)
            outs.append((_dot(pc.astype(BF16), vc) + _dot(pp.astype(BF16), vp)) / l)
            lses.append(m + jnp.log(l))
        o_ref[:, sl] = jnp.where(lo, outs[0], outs[1])
        lse_ref[:, sl] = jnp.where(lo, lses[0], lses[1])


def _dil_band(q, kv, dil):
    b, t, _ = q.shape
    ls = t // dil
    nbk = ls // DIL_BLOCK
    qv = q.reshape(b, ls, dil * HD)
    kvv = kv.reshape(b, ls, dil * 2 * HD)
    blk = lambda f: pl.BlockSpec((None, DIL_BLOCK, HD), f)
    cur = lambda off: (lambda bi, r, i: (bi, i, 2 * r + off))
    prev = lambda off: (lambda bi, r, i: (bi, jnp.maximum(i - 1, 0), 2 * r + off))
    o, lse = pl.pallas_call(
        _dil_body,
        out_shape=[jax.ShapeDtypeStruct((b, ls, dil * HD), F32)] * 2,
        grid=(b, dil, nbk),
        in_specs=[blk(lambda bi, r, i: (bi, i, r)), blk(prev(0)), blk(cur(0)), blk(prev(1)), blk(cur(1))],
        out_specs=[blk(lambda bi, r, i: (bi, i, r))] * 2,
        compiler_params=_cp("arbitrary", "arbitrary", "arbitrary"),
        name="dil_band",
    )(qv, kvv, kvv, kvv, kvv)
    return o.reshape(b, t, HD), lse.reshape(b, t, HD)


APG = 4


def _head_diag(rows):
    return (_iota((rows, HD), 1) >> 6) == _iota((rows, HD), 0)


def _paged_attn_body(pt_ref, q_ref, *refs, bias_heads):
    pages = refs[:APG]
    bias_ref, kvn_ref, bn_ref, o_ref, m_scr, l_scr, acc_scr = refs[APG:]
    s_idx = pl.program_id(1)
    diag = _head_diag(N_HEADS)
    qf = jnp.where(diag, jnp.broadcast_to(q_ref[...].astype(F32), (N_HEADS, HD)), 0.0)
    qbd = qf.astype(BF16)

    @pl.when(s_idx == 0)
    def _():
        kn = kvn_ref[:, 0:HD]
        s_new = jnp.sum(qf * kn, axis=1, keepdims=True) + bn_ref[:, 0:1]
        m_scr[...] = jnp.broadcast_to(s_new, m_scr.shape)
        l_scr[...] = jnp.ones_like(l_scr)
        acc_scr[...] = jnp.broadcast_to(kvn_ref[:, HD:2 * HD], acc_scr.shape)

    m = m_scr[:, 0:1]
    l = l_scr[:, 0:1]
    acc = acc_scr[...]
    for p in range(APG):
        pg = pages[p]
        s = _dot_nt(qbd, pg[:, 0:HD].astype(BF16)) + bias_ref[:, p * PAGE_SIZE:(p + 1) * PAGE_SIZE]
        mn = jnp.maximum(m, jnp.max(s, axis=1, keepdims=True))
        alpha = jnp.exp(m - mn)
        pr = jnp.exp(s - mn)
        l = alpha * l + jnp.sum(pr, axis=1, keepdims=True)
        acc = alpha * acc + _dot(pr.astype(BF16), pg[:, HD:2 * HD].astype(BF16))
        m = mn
    m_scr[...] = jnp.broadcast_to(m, m_scr.shape)
    l_scr[...] = jnp.broadcast_to(l, l_scr.shape)
    acc_scr[...] = acc

    @pl.when(s_idx == pl.num_programs(1) - 1)
    def _():
        o = jnp.where(diag, acc / l, 0.0)
        o_ref[...] = jnp.sum(o, axis=0, keepdims=True).astype(BF16)


def _paged_attn(page_table, q, cache_kv, bias, kv_new, bias_new):
    bs, n_pages = page_table.shape
    pool = cache_kv.shape[0]
    cache = cache_kv.reshape(pool, PAGE_SIZE, 2 * HD)
    bh = bias.shape[1]
    page_spec = lambda p: pl.BlockSpec(
        (None, PAGE_SIZE, 2 * HD), lambda bi, s, pt: (pt[bi * n_pages + s * APG + p], 0, 0))
    grid_spec = pltpu.PrefetchScalarGridSpec(
        num_scalar_prefetch=1,
        grid=(bs, n_pages // APG),
        in_specs=[pl.BlockSpec((None, 1, HD), lambda bi, s, pt: (bi, 0, 0))]
        + [page_spec(p) for p in range(APG)]
        + [pl.BlockSpec((None, bh, APG * PAGE_SIZE), lambda bi, s, pt: (bi, 0, s)),
           pl.BlockSpec((None, 1, 2 * HD), lambda bi, s, pt: (bi, 0, 0)),
           pl.BlockSpec((None, bh, LANES), lambda bi, s, pt: (bi, 0, 0))],
        out_specs=pl.BlockSpec((None, 1, HD), lambda bi, s, pt: (bi, 0, 0)),
        scratch_shapes=[pltpu.VMEM((N_HEADS, LANES), F32), pltpu.VMEM((N_HEADS, LANES), F32),
                        pltpu.VMEM((N_HEADS, HD), F32)],
    )
    out = pl.pallas_call(
        functools.partial(_paged_attn_body, bias_heads=bh),
        out_shape=jax.ShapeDtypeStruct((bs, 1, HD), BF16),
        grid_spec=grid_spec,
        compiler_params=_cp("arbitrary", "arbitrary"),
        name="paged_attn",
    )(page_table.reshape(-1), q.reshape(bs, 1, HD), *([cache] * APG), bias,
      kv_new.reshape(bs, 1, 2 * HD), bias_new)
    return out.reshape(1, bs, HD)


def _moba_gate_body(pt_ref, q_ref, p0_ref, p1_ref, bias_ref, gate_scr):
    n = pl.program_id(1)
    nb = pl.num_programs(1)
    past = bias_ref.shape[1]
    ksum = jnp.sum(p0_ref[...], axis=0, keepdims=True) + jnp.sum(p1_ref[...], axis=0, keepdims=True)
    kmean = ksum * (1.0 / MOBA_BLOCK)
    diag = _head_diag(N_HEADS)
    qbd = jnp.where(diag, jnp.broadcast_to(q_ref[...].astype(F32), (N_HEADS, HD)), 0.0)
    g = jnp.sum(qbd * kmean, axis=1, keepdims=True)
    lane = _iota((N_HEADS, LANES), 1)

    @pl.when(n == 0)
    def _():
        gate_scr[...] = jnp.full(gate_scr.shape, NEG_INF, F32)

    gate_scr[...] = jnp.where(lane == n, g, gate_scr[...])

    @pl.when(n == nb - 1)
    def _():
        i1, i2, i3 = _top3_lanes(gate_scr[...], lane.astype(F32))
        blk = (_iota((N_HEADS, past), 1) // MOBA_BLOCK).astype(F32)
        bias_ref[...] = jnp.where((blk == i1) | (blk == i2) | (blk == i3), 0.0, NEG_INF)


def _moba_gate(page_table, q, cache_kv):
    bs, n_pages = page_table.shape
    pool = cache_kv.shape[0]
    cache = cache_kv.reshape(pool, PAGE_SIZE, 2 * HD)
    ppb = MOBA_BLOCK // PAGE_SIZE
    past = n_pages * PAGE_SIZE
    page_spec = lambda p: pl.BlockSpec(
        (None, PAGE_SIZE, HD), lambda bi, n, pt: (pt[bi * n_pages + n * ppb + p], 0, 0))
    grid_spec = pltpu.PrefetchScalarGridSpec(
        num_scalar_prefetch=1,
        grid=(bs, n_pages // ppb),
        in_specs=[pl.BlockSpec((None, 1, HD), lambda bi, n, pt: (bi, 0, 0)), page_spec(0), page_spec(1)],
        out_specs=pl.BlockSpec((None, N_HEADS, past), lambda bi, n, pt: (bi, 0, 0)),
        scratch_shapes=[pltpu.VMEM((N_HEADS, LANES), F32)],
    )
    return pl.pallas_call(
        _moba_gate_body,
        out_shape=jax.ShapeDtypeStruct((bs, N_HEADS, past), F32),
        grid_spec=grid_spec,
        compiler_params=_cp("arbitrary", "arbitrary"),
        name="moba_gate",
    )(page_table.reshape(-1), q.reshape(bs, 1, HD), cache, cache)


SPG = 8


def _dsa_score_body(pt_ref, qi_ref, wi_ref, *refs):
    pages = refs[:SPG]
    out_ref = refs[SPG]
    qi = qi_ref[...]
    wi = wi_ref[...]
    for p in range(SPG):
        d = _dot_nt(qi, pages[p][...].astype(BF16))
        out_ref[:, p * PAGE_SIZE:(p + 1) * PAGE_SIZE] = jnp.sum(wi * jnp.maximum(d, 0.0), axis=0, keepdims=True)


def _dsa_scores(page_table, qi, wi, cache_kidx):
    bs, n_pages = page_table.shape
    page_spec = lambda p: pl.BlockSpec(
        (None, PAGE_SIZE, DSA_IDX_DIM), lambda bi, s, pt: (pt[bi * n_pages + s * SPG + p], 0, 0))
    grid_spec = pltpu.PrefetchScalarGridSpec(
        num_scalar_prefetch=1,
        grid=(bs, n_pages // SPG),
        in_specs=[pl.BlockSpec((None, DSA_IDX_HEADS, DSA_IDX_DIM), lambda bi, s, pt: (bi, 0, 0)),
                  pl.BlockSpec((None, DSA_IDX_HEADS, 1), lambda bi, s, pt: (bi, 0, 0))]
        + [page_spec(p) for p in range(SPG)],
        out_specs=pl.BlockSpec((None, 1, SPG * PAGE_SIZE), lambda bi, s, pt: (bi, 0, s)),
    )
    return pl.pallas_call(
        _dsa_score_body,
        out_shape=jax.ShapeDtypeStruct((bs, 1, n_pages * PAGE_SIZE), F32),
        grid_spec=grid_spec,
        compiler_params=_cp("arbitrary", "arbitrary"),
        name="dsa_scores",
    )(page_table.reshape(-1), qi, wi, *([cache_kidx] * SPG))


def _dsa_select_body(sc_ref, qi_ref, kiwi_ref, bias_ref, bn_ref, key_scr, lim_scr, *, n_keep):
    bs, past = sc_ref.shape
    nch = past // ICH
    for c in range(nch):
        key_scr[:, c * ICH:(c + 1) * ICH] = _sort_key(sc_ref[:, c * ICH:(c + 1) * ICH])
    kiwi = kiwi_ref[...]
    qi = qi_ref[...].astype(F32)
    s_new = jnp.zeros((bs, 1), F32)
    for h in range(DSA_IDX_HEADS):
        d = jnp.sum(qi[:, h * DSA_IDX_DIM:(h + 1) * DSA_IDX_DIM] * kiwi[:, 0:DSA_IDX_DIM],
                    axis=1, keepdims=True)
        s_new = s_new + kiwi[:, DSA_IDX_DIM + h:DSA_IDX_DIM + h + 1] * jnp.maximum(d, 0.0)
    key_new = _sort_key(s_new)

    kf = float(n_keep)
    thr = _kth_largest_key(key_scr, nch, kf, extra_key=key_new)
    n_gt = _count(key_scr, nch, lambda kc, st: kc > thr) + jnp.where(key_new > thr, 1.0, 0.0)
    n_eq = _count(key_scr, nch, lambda kc, st: kc == thr)
    need = kf - n_gt
    lim_scr[...] = jnp.full(lim_scr.shape, past, I32)
    has_tie = jnp.max(jnp.where(n_eq > need, 1.0, 0.0)) > 0.5

    @pl.when(has_tie)
    def _():
        _tie_limit(key_scr, nch, thr, need, past.bit_length(), lim_scr)

    lim = lim_scr[:, 0:1]
    for c in range(nch):
        kc = key_scr[:, c * ICH:(c + 1) * ICH]
        kpos = c * ICH + _iota((bs, ICH), 1)
        tie_ok = jnp.where(kc == thr, jnp.where(kpos <= lim, 0.0, NEG_INF), NEG_INF)
        bias_ref[:, c * ICH:(c + 1) * ICH] = jnp.where(kc > thr, 0.0, tie_ok)
    new_ok = (key_new > thr) | ((key_new == thr) & (n_eq < need))
    bn_ref[...] = jnp.broadcast_to(jnp.where(new_ok, 0.0, NEG_INF), bn_ref.shape)


def _dsa_select(scores, qi, kiwi, n_keep):
    bs, past = scores.shape
    return pl.pallas_call(
        functools.partial(_dsa_select_body, n_keep=n_keep),
        out_shape=[jax.ShapeDtypeStruct((bs, past), F32), jax.ShapeDtypeStruct((bs, LANES), F32)],
        scratch_shapes=[pltpu.VMEM((bs, past), I32), pltpu.VMEM((bs, LANES), I32)],
        compiler_params=pltpu.CompilerParams(vmem_limit_bytes=VMEM_LIMIT),
        name="dsa_select",
    )(scores, qi, kiwi)


FPG = 8


def _fox_bias_body(pt_ref, lfn_ref, *refs):
    pages = refs[:FPG]
    bias_ref, carry_scr = refs[FPG:]

    @pl.when(pl.program_id(1) == 0)
    def _():
        carry_scr[...] = lfn_ref[...]

    later = (_iota((PAGE_SIZE, PAGE_SIZE), 0) > _iota((PAGE_SIZE, PAGE_SIZE), 1)).astype(F32)
    carry = carry_scr[...]
    for p in range(FPG - 1, -1, -1):
        x = pages[p][...]
        bias_ref[:, p * PAGE_SIZE:(p + 1) * PAGE_SIZE] = _dot(x, later, precision=HIGHEST) + carry
        carry = carry + jnp.sum(x, axis=1, keepdims=True)
    carry_scr[...] = carry


def _fox_bias(page_table, logf_t, lf_new):
    bs, n_pages = page_table.shape
    nst = n_pages // FPG
    page_spec = lambda p: pl.BlockSpec(
        (None, N_HEADS, PAGE_SIZE),
        lambda bi, s, pt: (pt[bi * n_pages + (nst - 1 - s) * FPG + p], 0, 0))
    grid_spec = pltpu.PrefetchScalarGridSpec(
        num_scalar_prefetch=1,
        grid=(bs, nst),
        in_specs=[pl.BlockSpec((None, N_HEADS, LANES), lambda bi, s, pt: (bi, 0, 0))]
        + [page_spec(p) for p in range(FPG)],
        out_specs=pl.BlockSpec((None, N_HEADS, FPG * PAGE_SIZE), lambda bi, s, pt: (bi, 0, nst - 1 - s)),
        scratch_shapes=[pltpu.VMEM((N_HEADS, LANES), F32)],
    )
    return pl.pallas_call(
        _fox_bias_body,
        out_shape=jax.ShapeDtypeStruct((bs, N_HEADS, n_pages * PAGE_SIZE), F32),
        grid_spec=grid_spec,
        compiler_params=_cp("arbitrary", "arbitrary"),
        name="fox_bias",
    )(page_table.reshape(-1), lf_new, *([logf_t] * FPG))


def _dil_sample_body(*refs):
    n_grp = len(DIL_WINDOWS)
    q_refs = refs[:n_grp]
    kvn_refs = refs[n_grp:2 * n_grp]
    win_refs = refs[2 * n_grp:3 * n_grp]
    o_ref = refs[3 * n_grp]
    diag = _head_diag(N_HEADS)
    outs, lses = [], []
    for g in range(n_grp):
        qf = jnp.where(diag, jnp.broadcast_to(q_refs[g][...].astype(F32), (N_HEADS, HD)), 0.0)
        qbd = qf.astype(BF16)
        win = win_refs[g]
        s = _dot_nt(qbd, win[:, 0:HD].astype(BF16))
        s_new = jnp.sum(qf * kvn_refs[g][:, 0:HD], axis=1, keepdims=True)
        m = jnp.maximum(jnp.max(s, axis=1, keepdims=True), s_new)
        p = jnp.exp(s - m)
        p_new = jnp.exp(s_new - m)
        l = jnp.sum(p, axis=1, keepdims=True) + p_new
        o = _dot(p.astype(BF16), win[:, HD:2 * HD].astype(BF16)) + p_new * kvn_refs[g][:, HD:2 * HD]
        outs.append(o / l)
        lses.append(m + jnp.log(l))
    m = functools.reduce(jnp.maximum, lses)
    es = [jnp.exp(l - m) for l in lses]
    den = functools.reduce(lambda a, b: a + b, es)
    y = functools.reduce(lambda a, b: a + b, [(e / den) * o for e, o in zip(es, outs)])
    o_ref[...] = jnp.sum(jnp.where(diag, y, 0.0), axis=0, keepdims=True).astype(BF16)


def _dil_sample(qs, kvns, bufs):
    bs = qs[0].shape[0]
    args, specs = [], []
    for q in qs:
        args.append(q.reshape(bs, 1, HD))
        specs.append(pl.BlockSpec((None, 1, HD), lambda bi: (bi, 0, 0)))
    for kvn in kvns:
        args.append(kvn.reshape(bs, 1, 2 * HD))
        specs.append(pl.BlockSpec((None, 1, 2 * HD), lambda bi: (bi, 0, 0)))
    for buf, dil in zip(bufs, DIL_DILATIONS):
        w = buf.shape[1]
        args.append(buf.reshape(bs, w // dil, dil * 2 * HD))
        specs.append(pl.BlockSpec((None, w // dil, 2 * HD), lambda bi: (bi, 0, 0)))
    out = pl.pallas_call(
        _dil_sample_body,
        out_shape=jax.ShapeDtypeStruct((bs, 1, HD), BF16),
        grid=(bs,),
        in_specs=specs,
        out_specs=pl.BlockSpec((None, 1, HD), lambda bi: (bi, 0, 0)),
        compiler_params=_cp("arbitrary"),
        name="dil_sample",
    )(*args)
    return out.reshape(1, bs, HD)


WSR = 128
SUBLANES = 8


def _win_shift_body(cur_ref, nxt_ref, new_ref, out_ref):
    r = cur_ref.shape[0]
    out_ref[0:r - 1, :] = cur_ref[1:r, :]
    last = pl.program_id(1) == pl.num_programs(1) - 1

    @pl.when(last)
    def _():
        out_ref[r - 1:r, :] = new_ref[...]

    @pl.when(jnp.logical_not(last))
    def _():
        out_ref[r - 1:r, :] = nxt_ref[0:1, :]


def _win_shift(buf, kv_new):
    bs, w = buf.shape[:2]
    nst = w // WSR
    b3 = buf.reshape(bs, w, 2 * HD)
    out = pl.pallas_call(
        _win_shift_body,
        out_shape=jax.ShapeDtypeStruct((bs, w, 2 * HD), F32),
        grid=(bs, nst),
        in_specs=[pl.BlockSpec((None, WSR, 2 * HD), lambda bi, i: (bi, i, 0)),
                  pl.BlockSpec((None, SUBLANES, 2 * HD),
                               lambda bi, i: (bi, jnp.minimum(i + 1, nst - 1) * (WSR // SUBLANES), 0)),
                  pl.BlockSpec((None, 1, 2 * HD), lambda bi, i: (bi, 0, 0))],
        out_specs=pl.BlockSpec((None, WSR, 2 * HD), lambda bi, i: (bi, i, 0)),
        compiler_params=_cp("arbitrary", "arbitrary"),
        name="win_shift",
    )(b3, b3, kv_new.reshape(bs, 1, 2 * HD))
    return out.reshape(buf.shape)


def _rope_tables(pos):
    half = HEAD_DIM // 2
    inv = ROPE_THETA ** (-jnp.arange(half, dtype=F32) / half)
    ang = pos.astype(F32)[:, None] * inv
    cos, sin = jnp.cos(ang), jnp.sin(ang)
    return (jnp.concatenate([cos, cos, cos, cos], -1), jnp.concatenate([-sin, sin, -sin, sin], -1))


def _pad_cols(w, n):
    return jnp.pad(w, ((0, 0), (0, n - w.shape[1])))


def _kv5(kv):
    return kv.reshape(kv.shape[0], kv.shape[1], 2, N_HEADS, HEAD_DIM)


def kernel(x_prompt, x_sample, c_prompt, c_sample, cache_kv_moba, cache_kv_dsa, cache_kidx_dsa, cache_kv_fox, cache_logf_fox, state_win_d1, state_win_d4, state_win_d16, page_table, w_ada, b_ada, ln_g, ln_b, ffn_w_up, ffn_w_down, moba_w_qkv, moba_w_o, dsa_w_in, dsa_w_o, fox_w_in, fox_b_f, fox_w_o, dil_w_qkv, dil_w_o):
    bp, seq, d = x_prompt.shape
    bs = x_sample.shape[0]
    past = page_table.shape[1] * PAGE_SIZE
    depth = w_ada.shape[0]

    mp = -(-(bp + bs) // 8) * 8
    c_all = jnp.pad(jnp.concatenate([c_prompt, c_sample], 0), ((0, mp - bp - bs), (0, 0)))
    ada_all = _ada_all(c_all, w_ada, b_ada)
    ada_p = ada_all[:, :bp].reshape(depth, bp, N_ADA, 1, d)
    ada_s = ada_all[:, bp:bp + bs]

    cos_p, sin_p = _rope_tables(jnp.arange(seq, dtype=I32))
    cos_s, sin_s = _rope_tables(jnp.full((1,), past, I32))
    one_c, zero_s = jnp.ones((1, LANES), F32), jnp.zeros((1, LANES), F32)

    w_up = ffn_w_up.astype(BF16)
    w_down = ffn_w_down.astype(BF16)
    nqi = DSA_IDX_HEADS * DSA_IDX_DIM
    w_moba = moba_w_qkv.astype(BF16)
    w_dsa = _pad_cols(dsa_w_in, 3 * HD + nqi + LANES).astype(BF16)
    w_fox = _pad_cols(fox_w_in, 3 * HD + LANES).astype(BF16)
    w_dil = dil_w_qkv.astype(BF16)
    b_fox = jnp.pad(fox_b_f, (0, LANES - N_HEADS)).reshape(1, LANES)
    w_outs = [w.astype(BF16) for w in (moba_w_o, dsa_w_o, fox_w_o, dil_w_o)]

    xp = x_prompt
    xs = x_sample.reshape(1, bs, d)
    res = {}
    for i in range(depth):
        ap, as_ = ada_p[i], ada_s[i]
        xp = _ffn(xp, ap, 0, w_up[i, 0], w_down[i, 0], ln_g[i, 0], ln_b[i, 0], False)
        xs = _ffn(xs, as_, 0, w_up[i, 0], w_down[i, 0], ln_g[i, 0], ln_b[i, 0], True)
        kind = i % 4
        if kind == 0:
            qp, kvp = _proj(xp, ap, w_moba, cos_p, sin_p, False, True)
            op = [_flash(qp, kvp, "moba")]
            qs, kvs = _proj(xs, as_, w_moba, cos_s, sin_s, True, True)
            bias = _moba_gate(page_table, qs[0], cache_kv_moba)
            os_ = [_paged_attn(page_table, qs[0], cache_kv_moba, bias, kvs[0], jnp.zeros((bs, N_HEADS, LANES), F32))]
            res["kv_moba"] = (kvp, kvs)
            lp = ls = ()
        elif kind == 1:
            qp, kvp, qip, kiwip = _proj(xp, ap, w_dsa, cos_p, sin_p, False, True, "dsa")
            mask = _dsa_index(qip, kiwip, min(DSA_TOPK_MAX, seq // 4))
            op = [_flash(qp, kvp, "dsa", (mask,))]
            qs, kvs, qis, kiwis = _proj(xs, as_, w_dsa, cos_s, sin_s, True, True, "dsa")
            wi = kiwis[0, :, DSA_IDX_DIM:DSA_IDX_DIM + DSA_IDX_HEADS].reshape(bs, DSA_IDX_HEADS, 1)
            scores = _dsa_scores(page_table, qis[0].reshape(bs, DSA_IDX_HEADS, DSA_IDX_DIM), wi, cache_kidx_dsa)
            bias, bias_new = _dsa_select(scores.reshape(bs, past), qis[0], kiwis[0],
                                         min(DSA_TOPK_MAX, (past + 1) // 4))
            os_ = [_paged_attn(page_table, qs[0], cache_kv_dsa, bias.reshape(bs, 1, past), kvs[0],
                               bias_new.reshape(bs, 1, LANES))]
            res["kv_dsa"] = (kvp, kvs)
            res["kidx_dsa"] = (kiwip[..., :DSA_IDX_DIM], kiwis[..., :DSA_IDX_DIM].reshape(bs, 1, DSA_IDX_DIM))
            lp = ls = ()
        elif kind == 2:
            qp, kvp, lfp, fcp = _proj(xp, ap, w_fox, one_c, zero_s, False, False, "fox", b_fox)
            fct = fcp[..., :N_HEADS].transpose(0, 2, 1)
            op = [_flash(qp, kvp, "fox", (fcp, fct))]
            qs, kvs, lfs, _ = _proj(xs, as_, w_fox, one_c, zero_s, True, False, "fox", b_fox)
            lf_new = jnp.broadcast_to(lfs[0, :, :N_HEADS, None], (bs, N_HEADS, LANES))
            bias = _fox_bias(page_table, cache_logf_fox.transpose(0, 2, 1), lf_new)
            os_ = [_paged_attn(page_table, qs[0], cache_kv_fox, bias, kvs[0], jnp.zeros((bs, N_HEADS, LANES), F32))]
            res["kv_fox"] = (kvp, kvs)
            res["logf_fox"] = (lfp[..., :N_HEADS], lfs[..., :N_HEADS].reshape(bs, 1, N_HEADS))
            lp = ls = ()
        else:
            op, lp, q_s, kv_s = [], [], [], []
            bufs = (state_win_d1, state_win_d4, state_win_d16)
            for g, (win, dil) in enumerate(zip(DIL_WINDOWS, DIL_DILATIONS)):
                wg = w_dil[:, g * 3 * HD:(g + 1) * 3 * HD]
                qp, kvp = _proj(xp, ap, wg, cos_p, sin_p, False, True)
                o, lse = _dil_band(qp, kvp, dil)
                op.append(o)
                lp.append(lse)
                qs, kvs = _proj(xs, as_, wg, cos_s, sin_s, True, True)
                q_s.append(qs[0])
                kv_s.append(kvs[0])
                res["win_%d" % g] = (_kv5(kvp[:, seq - min(win, seq):]), _win_shift(bufs[g], kvs[0]))
            os_ = [_dil_sample(q_s, kv_s, bufs)]
            ls = ()
        xp = _oproj(xp, ap, op, lp, w_outs[kind], ln_g[i, 1], ln_b[i, 1], False)
        xs = _oproj(xs, as_, os_, ls, w_outs[kind], ln_g[i, 1], ln_b[i, 1], True)
        xp = _ffn(xp, ap, 2, w_up[i, 1], w_down[i, 1], ln_g[i, 2], ln_b[i, 2], False)
        xs = _ffn(xs, as_, 2, w_up[i, 1], w_down[i, 1], ln_g[i, 2], ln_b[i, 2], True)

    kv_out = lambda name: (_kv5(res[name][0]), _kv5(res[name][1].reshape(bs, 1, 2 * HD)))
    return (xp, xs.reshape(bs, 1, d),
            *kv_out("kv_moba"), *kv_out("kv_dsa"), *res["kidx_dsa"],
            *kv_out("kv_fox"), *res["logf_fox"],
            *res["win_0"], *res["win_1"], *res["win_2"])
```

```python
import functools

import jax
import jax.numpy as jnp
from jax import lax
from jax.experimental import pallas as pl
from jax.experimental.pallas import tpu as pltpu

F32 = jnp.float32
BF16 = jnp.bfloat16
I32 = jnp.int32
HIGHEST = lax.Precision.HIGHEST

D_MODEL = 1024
N_HEADS = 16
HEAD_DIM = 64
HALF = HEAD_DIM // 2
HD = N_HEADS * HEAD_DIM
D_FF = 2816
N_ADA = 9
DEPTH = 4
PAGE_SIZE = 128
FFN_RES_WEIGHT = 0.5
ROPE_THETA = 10000.0
LN_EPS = 1e-5
NEG_INF = -1e30
DN_ALPHA = (2 * DEPTH) ** 0.25
MOBA_BLOCK = 256
MOBA_TOPK = 3
DSA_TOPK_MAX = 256
DSA_IDX_HEADS = 8
DSA_IDX_DIM = 64
DSA_NQI = DSA_IDX_HEADS * DSA_IDX_DIM
DSA_W_SCALE = (DSA_IDX_HEADS ** -0.5) * (DSA_IDX_DIM ** -0.5)
DIL_WINDOWS = (128, 512, 2048)
DIL_DILATIONS = (1, 4, 16)
DIL_BLOCK = 128
QK_SCALE = HEAD_DIM ** -0.5

LANES = 128
SUBLANES = 8
VMEM_LIMIT = 56 * 2 ** 20
INT_MIN = -2 ** 31
NEG_KEY = -1900671691

_NT = (((1,), (1,)), ((), ()))


def _cp(*sem):
    return pltpu.CompilerParams(dimension_semantics=sem, vmem_limit_bytes=VMEM_LIMIT)


def _dot(a, b, **kw):
    return jnp.dot(a, b, preferred_element_type=F32, **kw)


def _dot_nt(a, b, **kw):
    return lax.dot_general(a, b, _NT, preferred_element_type=F32, **kw)


def _iota(shape, dim):
    return lax.broadcasted_iota(I32, shape, dim)


def _sort_key(x):
    x = jnp.where(x == 0.0, 0.0, x)
    bits = pltpu.bitcast(x, I32)
    return jnp.where(bits < 0, bits ^ 0x7FFFFFFF, bits)


def _log_sigmoid(z):
    return jnp.minimum(z, 0.0) - jnp.log1p(jnp.exp(-jnp.abs(z)))


def _post_ln(x, y, gate, g, b, weight):
    r = DN_ALPHA * x + (weight * (1.0 + gate)) * y
    mu = jnp.mean(r, axis=-1, keepdims=True)
    rc = r - mu
    var = jnp.mean(rc * rc, axis=-1, keepdims=True)
    return rc * lax.rsqrt(var + LN_EPS) * g + b


def _ada_operand(ada, j, k, per_row):
    if per_row:
        m = ada.shape[0]
        return ada, pl.BlockSpec((m, D_MODEL), lambda *g: (0, 3 * j + k))
    return ada, pl.BlockSpec((None, None, 1, D_MODEL), lambda *g: (g[0], 3 * j + k, 0, 0))


def _ada_body(c_ref, w_ref, b_ref, o_ref):
    o_ref[...] = _dot(c_ref[...].astype(BF16), w_ref[...].astype(BF16)) + b_ref[...]


def _ada_all(c_all, w_ada, b_ada):
    m = c_all.shape[0]
    depth, d, n = w_ada.shape
    tn = 1152
    return pl.pallas_call(
        _ada_body,
        out_shape=jax.ShapeDtypeStruct((depth, m, n), F32),
        grid=(depth, n // tn),
        in_specs=[pl.BlockSpec((m, d), lambda l, c: (0, 0)),
                  pl.BlockSpec((None, d, tn), lambda l, c: (l, 0, c)),
                  pl.BlockSpec((None, 1, tn), lambda l, c: (l, 0, c))],
        out_specs=pl.BlockSpec((None, m, tn), lambda l, c: (l, 0, c)),
        compiler_params=_cp("arbitrary", "arbitrary"),
        name="ada",
    )(c_all, w_ada, b_ada.reshape(depth, 1, n))


def _ffn_body(x_ref, sh_ref, sc_ref, gt_ref, wg_ref, wu_ref, wd_ref, lg_ref, lb_ref, o_ref,
              h_scr, acc_scr):
    f = pl.program_id(2)

    @pl.when(f == 0)
    def _():
        h_scr[...] = (x_ref[...] * (1.0 + sc_ref[...]) + sh_ref[...]).astype(BF16)

    h = h_scr[...]
    g = _dot(h, wg_ref[...])
    u = _dot(h, wu_ref[...])
    a = (g * jax.nn.sigmoid(g) * u).astype(BF16)
    y = _dot(a, wd_ref[...])

    @pl.when(f == 0)
    def _():
        acc_scr[...] = y

    @pl.when(f > 0)
    def _():
        acc_scr[...] += y

    @pl.when(f == pl.num_programs(2) - 1)
    def _():
        o_ref[...] = _post_ln(x_ref[...], acc_scr[...], gt_ref[...], lg_ref[...], lb_ref[...],
                              FFN_RES_WEIGHT)


def _ffn(x, ada, j, w_up, w_down, ln_g, ln_b, per_row):
    b, t, d = x.shape
    ff = w_down.shape[0]
    tm = min(t, 512)
    fc = 1408 if ff % 1408 == 0 else ff
    nf = ff // fc
    sh, sh_spec = _ada_operand(ada, j, 0, per_row)
    sc, sc_spec = _ada_operand(ada, j, 1, per_row)
    gt, gt_spec = _ada_operand(ada, j, 2, per_row)
    return pl.pallas_call(
        _ffn_body,
        out_shape=jax.ShapeDtypeStruct((b, t, d), F32),
        grid=(b, t // tm, nf),
        in_specs=[pl.BlockSpec((None, tm, d), lambda bi, i, f: (bi, i, 0)),
                  sh_spec, sc_spec, gt_spec,
                  pl.BlockSpec((d, fc), lambda bi, i, f: (0, f)),
                  pl.BlockSpec((d, fc), lambda bi, i, f: (0, f + nf)),
                  pl.BlockSpec((fc, d), lambda bi, i, f: (f, 0)),
                  pl.BlockSpec((1, d), lambda bi, i, f: (0, 0)),
                  pl.BlockSpec((1, d), lambda bi, i, f: (0, 0))],
        out_specs=pl.BlockSpec((None, tm, d), lambda bi, i, f: (bi, i, 0)),
        scratch_shapes=[pltpu.VMEM((tm, d), BF16), pltpu.VMEM((tm, d), F32)],
        compiler_params=_cp("arbitrary", "arbitrary", "arbitrary"),
        name="ffn",
    )(x, sh, sc, gt, w_up, w_up, w_down, ln_g.reshape(1, d), ln_b.reshape(1, d))


def _rope_lanes(zc, cos, sin, first_half):
    sw = jnp.where(first_half, pltpu.roll(zc, 96, 1), pltpu.roll(zc, 32, 1))
    return zc * cos + sw * sin


def _rope_rows_store(ref, row0, z, cos, sin, scale=None, dtype=F32):
    for h in range(z.shape[0] // HEAD_DIM):
        x1 = z[h * HEAD_DIM:h * HEAD_DIM + HALF, :]
        x2 = z[h * HEAD_DIM + HALF:(h + 1) * HEAD_DIM, :]
        o1 = x1 * cos - x2 * sin
        o2 = x2 * cos + x1 * sin
        if scale is not None:
            o1, o2 = o1 * scale, o2 * scale
        r = row0 + h * HEAD_DIM
        ref[r:r + HALF, :] = o1.astype(dtype)
        ref[r + HALF:r + HEAD_DIM, :] = o2.astype(dtype)


def _proj_body(*refs, rope, extra):
    x_ref, sh_ref, sc_ref, cos_ref, sin_ref, w_ref = refs[:6]
    rest = refs[6:]
    if extra == "fox":
        bf_ref, q_ref, kv_ref, lf_ref = rest
    elif extra == "dsa":
        q_ref, kv_ref, qi_ref, kiwi_ref = rest
    else:
        q_ref, kv_ref = rest
    tm = x_ref.shape[0]
    h = (x_ref[...] * (1.0 + sc_ref[...]) + sh_ref[...]).astype(BF16)
    cos = cos_ref[...]
    sin = sin_ref[...]
    lane = _iota((tm, LANES), 1)
    first_half = (lane & HALF) == 0

    zq = _dot(h, w_ref[:, 0:HD])
    for s in range(HD // LANES):
        sl = slice(s * LANES, (s + 1) * LANES)
        c = zq[:, sl]
        q_ref[:, sl] = ((_rope_lanes(c, cos, sin, first_half) if rope else c) * QK_SCALE).astype(q_ref.dtype)
    zk = _dot(h, w_ref[:, HD:2 * HD])
    for s in range(HD // LANES):
        sl = slice(s * LANES, (s + 1) * LANES)
        c = zk[:, sl]
        kv_ref[:, sl] = (_rope_lanes(c, cos, sin, first_half) if rope else c).astype(kv_ref.dtype)
    kv_ref[:, HD:2 * HD] = _dot(h, w_ref[:, 2 * HD:3 * HD]).astype(kv_ref.dtype)

    if extra == "dsa":
        ze = _dot(h, w_ref[:, 3 * HD:3 * HD + DSA_NQI + LANES])
        for s in range(DSA_NQI // LANES):
            sl = slice(s * LANES, (s + 1) * LANES)
            qi_ref[:, sl] = _rope_lanes(ze[:, sl], cos, sin, first_half).astype(BF16)
        c = ze[:, DSA_NQI:DSA_NQI + LANES]
        kiwi_ref[...] = jnp.where(lane < DSA_IDX_DIM, _rope_lanes(c, cos, sin, first_half), c * DSA_W_SCALE)
    elif extra == "fox":
        lf_ref[...] = _log_sigmoid(_dot(h, w_ref[:, 3 * HD:3 * HD + LANES]) + bf_ref[...])


def _proj(x, ada, w, cos, sin, per_row, rope, extra=None, b_f=None, qkv_dtype=(BF16, F32)):
    b, t, d = x.shape
    n = w.shape[1]
    tm = min(t, 512)
    sh, sh_spec = _ada_operand(ada, 1, 0, per_row)
    sc, sc_spec = _ada_operand(ada, 1, 1, per_row)
    tcs = cos.shape[0]
    cs_spec = (pl.BlockSpec((1, LANES), lambda bi, i: (0, 0)) if tcs == 1
               else pl.BlockSpec((tm, LANES), lambda bi, i: (i, 0)))
    row = lambda w_: pl.BlockSpec((None, tm, w_), lambda bi, i: (bi, i, 0))
    in_specs = [row(d), sh_spec, sc_spec, cs_spec, cs_spec,
                pl.BlockSpec((d, n), lambda bi, i: (0, 0))]
    args = [x, sh, sc, cos, sin, w]
    out_shape = [jax.ShapeDtypeStruct((b, t, HD), qkv_dtype[0]),
                 jax.ShapeDtypeStruct((b, t, 2 * HD), qkv_dtype[1])]
    out_specs = [row(HD), row(2 * HD)]
    if extra == "dsa":
        out_shape += [jax.ShapeDtypeStruct((b, t, DSA_NQI), BF16), jax.ShapeDtypeStruct((b, t, LANES), F32)]
        out_specs += [row(DSA_NQI), row(LANES)]
    elif extra == "fox":
        in_specs.append(pl.BlockSpec((1, LANES), lambda bi, i: (0, 0)))
        args.append(b_f)
        out_shape += [jax.ShapeDtypeStruct((b, t, LANES), F32)]
        out_specs += [row(LANES)]
    return pl.pallas_call(
        functools.partial(_proj_body, rope=rope, extra=extra),
        out_shape=out_shape,
        grid=(b, t // tm),
        in_specs=in_specs,
        out_specs=out_specs,
        compiler_params=_cp("arbitrary", "arbitrary"),
        name="proj_" + (extra or "qkv"),
    )(*args)


PROJT_TM = 512


def _projt_body(*refs, rope, extra):
    x_ref, sh_ref, sc_ref, cosn_ref, sinn_ref, cost_ref, sint_ref, wkn_ref, wt_ref = refs[:9]
    rest = refs[9:]
    if extra == "moba":
        qt_ref, kn_ref, kvt_ref, kmean_ref = rest
    elif extra == "dsa":
        wet_ref, wkin_ref, qt_ref, kn_ref, kvt_ref, qit_ref, kin_ref, kit_ref, wit_ref = rest
    else:
        wft_ref, wfn_ref, bft_ref, bfn_ref, qt_ref, kn_ref, kvt_ref, lft_ref, ft_ref, fn_ref, ct_scr, cn_scr = rest
    i = pl.program_id(1)
    tm = x_ref.shape[0]
    h = (x_ref[...] * (1.0 + sc_ref[...]) + sh_ref[...]).astype(BF16)
    lane = _iota((tm, LANES), 1)
    first_half = (lane & HALF) == 0
    if rope:
        cosn, sinn, cost, sint = cosn_ref[...], sinn_ref[...], cost_ref[...], sint_ref[...]

    zk = _dot(h, wkn_ref[...])
    kparts = []
    for s in range(HD // LANES):
        sl = slice(s * LANES, (s + 1) * LANES)
        c = zk[:, sl]
        c = _rope_lanes(c, cosn, sinn, first_half) if rope else c
        kn_ref[:, sl] = c.astype(BF16)
        kparts.append(c)
    if extra == "moba":
        kmean_ref[...] = jnp.zeros_like(kmean_ref)
        for nb in range(tm // MOBA_BLOCK):
            for s in range(HD // LANES):
                blk = kparts[s][nb * MOBA_BLOCK:(nb + 1) * MOBA_BLOCK, :]
                kmean_ref[nb:nb + 1, s * LANES:(s + 1) * LANES] = (
                    jnp.sum(blk, axis=0, keepdims=True) * (1.0 / MOBA_BLOCK))

    zq = _dot_nt(wt_ref[0:HD, :], h)
    if rope:
        _rope_rows_store(qt_ref, 0, zq, cost, sint, scale=QK_SCALE, dtype=BF16)
    else:
        qt_ref[...] = (zq * QK_SCALE).astype(BF16)
    zkt = _dot_nt(wt_ref[HD:2 * HD, :], h)
    if rope:
        _rope_rows_store(kvt_ref, 0, zkt, cost, sint)
    else:
        kvt_ref[0:HD, :] = zkt
    kvt_ref[HD:2 * HD, :] = _dot_nt(wt_ref[2 * HD:3 * HD, :], h)

    if extra == "dsa":
        ze = _dot_nt(wet_ref[...], h)
        _rope_rows_store(qit_ref, 0, ze[0:DSA_NQI, :], cost, sint, dtype=BF16)
        _rope_rows_store(kit_ref, 0, ze[DSA_NQI:DSA_NQI + DSA_IDX_DIM, :], cost, sint)
        wit_ref[...] = ze[DSA_NQI + DSA_IDX_DIM:, :] * DSA_W_SCALE
        zkin = _dot(h, wkin_ref[...])
        kin_ref[...] = jnp.where(lane < DSA_IDX_DIM, _rope_lanes(zkin, cosn, sinn, first_half), 0.0).astype(BF16)
    elif extra == "fox":
        @pl.when(i == 0)
        def _():
            ct_scr[...] = jnp.zeros_like(ct_scr)
            cn_scr[...] = jnp.zeros_like(cn_scr)

        lft = _log_sigmoid(_dot_nt(wft_ref[...], h) + bft_ref[...])
        lft_ref[...] = lft
        upper = (_iota((tm, tm), 0) <= _iota((tm, tm), 1)).astype(F32)
        ft = _dot(lft, upper, precision=HIGHEST) + ct_scr[:, 0:1]
        ft_ref[...] = ft
        ct_scr[...] = jnp.broadcast_to(ft[:, tm - 1:tm], ct_scr.shape)
        lfn = _log_sigmoid(_dot(h, wfn_ref[...]) + bfn_ref[...])
        lower = (_iota((tm, tm), 0) >= _iota((tm, tm), 1)).astype(F32)
        fn = _dot(lower, lfn, precision=HIGHEST) + cn_scr[...]
        fn_ref[...] = fn
        cn_scr[...] = fn[tm - 1:tm, :]


def _projt(x, ada, w, rope_tabs, rope, extra, b_f=None):
    b, t, d = x.shape
    tm = PROJT_TM
    cosn, sinn, cost, sint = rope_tabs
    sh, sh_spec = _ada_operand(ada, 1, 0, False)
    sc, sc_spec = _ada_operand(ada, 1, 1, False)
    wb = w.astype(BF16)
    wkn = wb[:, HD:2 * HD]
    wt = wb[:, 0:3 * HD].T
    full = lambda a: pl.BlockSpec(a.shape, lambda bi, i: (0,) * a.ndim)
    tok = lambda w_: pl.BlockSpec((None, tm, w_), lambda bi, i: (bi, i, 0))
    chan = lambda c_: pl.BlockSpec((None, c_, tm), lambda bi, i: (bi, 0, i))
    in_specs = [tok(d), sh_spec, sc_spec,
                pl.BlockSpec((tm, LANES), lambda bi, i: (i, 0)), pl.BlockSpec((tm, LANES), lambda bi, i: (i, 0)),
                pl.BlockSpec((HALF, tm), lambda bi, i: (0, i)), pl.BlockSpec((HALF, tm), lambda bi, i: (0, i)),
                full(wkn), full(wt)]
    args = [x, sh, sc, cosn, sinn, cost, sint, wkn, wt]
    out_shape = [jax.ShapeDtypeStruct((b, HD, t), BF16), jax.ShapeDtypeStruct((b, t, HD), BF16),
                 jax.ShapeDtypeStruct((b, 2 * HD, t), F32)]
    out_specs = [chan(HD), tok(HD), chan(2 * HD)]
    scratch = []
    if extra == "moba":
        out_shape.append(jax.ShapeDtypeStruct((b, t // tm, SUBLANES, HD), F32))
        out_specs.append(pl.BlockSpec((None, None, SUBLANES, HD), lambda bi, i: (bi, i, 0, 0)))
    elif extra == "dsa":
        wet = wb[:, 3 * HD:].T
        wkin = jnp.pad(wb[:, 3 * HD + DSA_NQI:3 * HD + DSA_NQI + DSA_IDX_DIM], ((0, 0), (0, LANES - DSA_IDX_DIM)))
        in_specs += [full(wet), full(wkin)]
        args += [wet, wkin]
        out_shape += [jax.ShapeDtypeStruct((b, DSA_NQI, t), BF16), jax.ShapeDtypeStruct((b, t, LANES), BF16),
                      jax.ShapeDtypeStruct((b, DSA_IDX_DIM, t), F32), jax.ShapeDtypeStruct((b, DSA_IDX_HEADS, t), F32)]
        out_specs += [chan(DSA_NQI), tok(LANES), chan(DSA_IDX_DIM), chan(DSA_IDX_HEADS)]
    else:
        wft = wb[:, 3 * HD:].T
        wfn = jnp.pad(wb[:, 3 * HD:], ((0, 0), (0, LANES - N_HEADS)))
        bft = jnp.broadcast_to(b_f.reshape(N_HEADS, 1), (N_HEADS, tm))
        bfn = jnp.pad(b_f, (0, LANES - N_HEADS)).reshape(1, LANES)
        in_specs += [full(wft), full(wfn), full(bft), full(bfn)]
        args += [wft, wfn, bft, bfn]
        out_shape += [jax.ShapeDtypeStruct((b, N_HEADS, t), F32), jax.ShapeDtypeStruct((b, N_HEADS, t), F32),
                      jax.ShapeDtypeStruct((b, t, LANES), F32)]
        out_specs += [chan(N_HEADS), chan(N_HEADS), tok(LANES)]
        scratch = [pltpu.VMEM((N_HEADS, LANES), F32), pltpu.VMEM((1, LANES), F32)]
    return pl.pallas_call(
        functools.partial(_projt_body, rope=rope, extra=extra),
        out_shape=out_shape,
        grid=(b, t // tm),
        in_specs=in_specs,
        out_specs=out_specs,
        scratch_shapes=scratch,
        compiler_params=_cp("arbitrary", "arbitrary"),
        name="projt_" + extra,
    )(*args)


def _oproj_body(*refs, n_grp):
    x_ref, gt_ref = refs[:2]
    o_refs = refs[2:2 + n_grp]
    l_refs = refs[2 + n_grp:2 + 2 * n_grp] if n_grp > 1 else ()
    w_ref, lg_ref, lb_ref, out_ref = refs[2 + len(o_refs) + len(l_refs):]
    if n_grp == 1:
        a = o_refs[0][...].astype(BF16)
    else:
        ls = [r[...] for r in l_refs]
        m = functools.reduce(jnp.maximum, ls)
        es = [jnp.exp(l - m) for l in ls]
        den = functools.reduce(lambda p, q: p + q, es)
        a = functools.reduce(lambda p, q: p + q, [(e / den) * r[...] for e, r in zip(es, o_refs)])
        a = a.astype(BF16)
    y = _dot(a, w_ref[...])
    out_ref[...] = _post_ln(x_ref[...], y, gt_ref[...], lg_ref[...], lb_ref[...], 1.0)


def _oproj(x, ada, outs, lses, w_o, ln_g, ln_b, per_row):
    b, t, d = x.shape
    tm = min(t, 512)
    gt, gt_spec = _ada_operand(ada, 1, 2, per_row)
    row = pl.BlockSpec((None, tm, d), lambda bi, i: (bi, i, 0))
    one = pl.BlockSpec((1, d), lambda bi, i: (0, 0))
    n_grp = len(outs)
    return pl.pallas_call(
        functools.partial(_oproj_body, n_grp=n_grp),
        out_shape=jax.ShapeDtypeStruct((b, t, d), F32),
        grid=(b, t // tm),
        in_specs=[row, gt_spec] + [row] * (n_grp + len(lses))
        + [pl.BlockSpec((HD, d), lambda bi, i: (0, 0)), one, one],
        out_specs=row,
        compiler_params=_cp("arbitrary", "arbitrary"),
        name="oproj",
    )(x, gt, *outs, *lses, w_o, ln_g.reshape(1, d), ln_b.reshape(1, d))


TQ = 256
TKC = 512


def _top3_rows(g, blk):
    idx = []
    for _ in range(MOBA_TOPK):
        m = jnp.max(g, axis=0, keepdims=True)
        ix = jnp.min(jnp.where(g == m, blk, 1e9), axis=0, keepdims=True)
        idx.append(ix)
        g = jnp.where(blk == ix, -3e38, g)
    return idx


def _flasht_body(*refs, mode):
    if mode == "moba":
        qt_ref, k_ref, vt_ref, km_ref, o_ref, s_scr = refs
    elif mode == "dsa":
        qt_ref, k_ref, vt_ref, mask_ref, o_ref, s_scr = refs
    else:
        qt_ref, k_ref, vt_ref, fn_ref, ft_ref, o_ref, s_scr = refs
    hp = pl.program_id(1)
    j = pl.program_id(2)
    top = _iota((LANES, TQ), 0) < HEAD_DIM
    qt = qt_ref[...].astype(F32)
    qh = (jnp.where(top, qt, 0.0).astype(BF16), jnp.where(top, 0.0, qt).astype(BF16))
    nck = (j + 2) // 2
    qpos = j * TQ + _iota((TKC, TQ), 1)
    first_blk = _iota((TKC, TQ), 0) < TQ

    if mode == "moba":
        km = km_ref[...]
        blk = _iota((km.shape[0], TQ), 0).astype(F32)
        jf = j.astype(F32)
        sel_idx = []
        for h in range(2):
            qf = jnp.where(top, qt, 0.0) if h == 0 else jnp.where(top, 0.0, qt)
            g = _dot(km, qf, precision=HIGHEST)
            sel_idx.append(_top3_rows(jnp.where(blk < jf, g, NEG_INF), blk))
    elif mode == "fox":
        fq = [ft_ref[pl.ds(2 * hp + h, 1), :] for h in range(2)]
        lane_k = _iota((TKC, LANES), 1)

    def scores(c, m_run, last):
        start = pl.multiple_of(c * TKC, TKC)
        kb = k_ref[pl.ds(start, TKC), :]
        if mode == "fox":
            f_tile = fn_ref[pl.ds(start, TKC), :]
        elif mode == "dsa":
            keep = mask_ref[pl.ds(start, TKC), :].astype(F32) > 0.5
        if last:
            causal = start + _iota((TKC, TQ), 0) <= qpos
        m_new = []
        for h in range(2):
            s = _dot(kb, qh[h])
            if mode == "fox":
                fk = jnp.sum(jnp.where(lane_k == 2 * hp + h, f_tile, 0.0), axis=1, keepdims=True)
                s = (s + fq[h]) - fk
            elif mode == "dsa":
                s = jnp.where(keep, s, NEG_INF)
            else:
                i1, i2, i3 = sel_idx[h]
                hits = []
                for half in range(2):
                    nf = (2 * c + half).astype(F32)
                    hit = jnp.where((i1 == nf) | (i2 == nf) | (i3 == nf), 1.0, 0.0)
                    if last:
                        hit = jnp.maximum(hit, jnp.where(nf >= jf, 1.0, 0.0))
                    hits.append(hit)
                s = jnp.where(jnp.where(first_blk, hits[0], hits[1]) > 0.5, s, NEG_INF)
            if last:
                s = jnp.where(causal, s, NEG_INF)
            s_scr[h, pl.ds(start, TKC), :] = s
            m_new.append(jnp.maximum(m_run[h], jnp.max(s.reshape(TKC // SUBLANES, SUBLANES, TQ), axis=0)))
        return tuple(m_new)

    def paired(fn, n, init):
        carry = lax.fori_loop(0, n // 2, lambda i, c: fn(2 * i + 1, fn(2 * i, c)), init)
        return lax.cond(n % 2 == 1, lambda c: fn(n - 1, c), lambda c: c, carry)

    neg = jnp.full((SUBLANES, TQ), -jnp.inf, F32)
    m_run = paired(lambda c, m: scores(c, m, False), nck - 1, (neg, neg))
    m_run = scores(nck - 1, m_run, True)
    m_fin = [jnp.max(m, axis=0, keepdims=True) for m in m_run]
    topk = _iota((LANES, TKC), 0) < HEAD_DIM

    def values(c, acc):
        start = pl.multiple_of(c * TKC, TKC)
        vb = vt_ref[:, pl.ds(start, TKC)]
        out = []
        for h in range(2):
            p = jnp.exp(s_scr[h, pl.ds(start, TKC), :] - m_fin[h]).astype(BF16)
            vext = (jnp.where(topk, vb, 1.0) if h == 0 else jnp.where(topk, 1.0, vb)).astype(BF16)
            out.append(acc[h] + _dot(vext, p))
        return tuple(out)

    acc0 = jnp.zeros((LANES, TQ), F32)
    acc = paired(values, nck, (acc0, acc0))
    l0 = acc[0][HEAD_DIM:HEAD_DIM + 1, :]
    l1 = acc[1][0:1, :]
    ot = jnp.where(top, acc[0] / l0, acc[1] / l1)
    o_ref[...] = ot.T.astype(BF16)


def _flasht(qt, kn, kvt, mode, extra=()):
    b, _, t = qt.shape
    assert t % TKC == 0, "the key chunks of the last query tile must stay inside the sequence"
    nq = t // TQ
    nhp = HD // LANES
    in_specs = [pl.BlockSpec((None, LANES, TQ), lambda bi, hp, j: (bi, hp, j)),
                pl.BlockSpec((None, t, LANES), lambda bi, hp, j: (bi, 0, hp)),
                pl.BlockSpec((None, LANES, t), lambda bi, hp, j: (bi, nhp + hp, 0))]
    if mode == "moba":
        nb = extra[0].shape[1]
        in_specs.append(pl.BlockSpec((None, nb, LANES), lambda bi, hp, j: (bi, 0, hp)))
    elif mode == "dsa":
        in_specs.append(pl.BlockSpec((None, t, TQ), lambda bi, hp, j: (bi, 0, j)))
    else:
        in_specs += [pl.BlockSpec((None, t, LANES), lambda bi, hp, j: (bi, 0, 0)),
                     pl.BlockSpec((None, N_HEADS, TQ), lambda bi, hp, j: (bi, 0, j))]
    return pl.pallas_call(
        functools.partial(_flasht_body, mode=mode),
        out_shape=jax.ShapeDtypeStruct((b, t, HD), BF16),
        grid=(b, nhp, nq),
        in_specs=in_specs,
        out_specs=pl.BlockSpec((None, TQ, LANES), lambda bi, hp, j: (bi, j, hp)),
        scratch_shapes=[pltpu.VMEM((2, t, TQ), F32)],
        compiler_params=_cp("arbitrary", "arbitrary", "arbitrary"),
        name="flash_" + mode,
    )(qt, kn, kvt, *extra)


ICH = 512


def _counter(key_scr, n_chunks, key_axis):
    nq = key_scr.shape[1 - key_axis]

    def count(pred):
        def body(c, cnt):
            st = pl.multiple_of(c * ICH, ICH)
            if key_axis == 1:
                kc = key_scr[:, pl.ds(st, ICH)]
                x = jnp.where(pred(kc, st + _iota(kc.shape, 1)), 1.0, 0.0)
                for s in range(ICH // LANES):
                    cnt = cnt + x[:, s * LANES:(s + 1) * LANES]
                return cnt
            kc = key_scr[pl.ds(st, ICH), :]
            x = jnp.where(pred(kc, st + _iota(kc.shape, 0)), 1.0, 0.0)
            return cnt + jnp.sum(x.reshape(ICH // SUBLANES, SUBLANES, nq), axis=0)

        if key_axis == 1:
            cnt = lax.fori_loop(0, n_chunks, body, jnp.zeros((nq, LANES), F32))
            return jnp.sum(cnt, axis=1, keepdims=True)
        cnt = lax.fori_loop(0, n_chunks, body, jnp.zeros((SUBLANES, nq), F32))
        return jnp.sum(cnt, axis=0, keepdims=True)

    return count


def _kth_largest_key(count, qshape, k, extra_key=None):
    def count_ge(cand):
        c = count(lambda kc, idx: kc >= cand)
        if extra_key is not None:
            c = c + jnp.where(extra_key >= cand, 1.0, 0.0)
        return c

    t0 = jnp.where(count_ge(jnp.zeros(qshape, I32)) >= k, 0, INT_MIN).astype(I32)

    def bit_body(it, cur):
        cand = cur + lax.shift_left(jnp.int32(1), 30 - it)
        return jnp.where(count_ge(cand) >= k, cand, cur)

    return lax.fori_loop(0, 31, bit_body, t0)


def _tie_limit(count, qshape, thr, need, n_bits):
    def bit_body(it, cur):
        cand = cur + lax.shift_left(jnp.int32(1), n_bits - 1 - it)
        c = count(lambda kc, idx: (kc == thr) & (idx < cand))
        return jnp.where(c < need, cand, cur)

    return lax.fori_loop(0, n_bits, bit_body, jnp.zeros(qshape, I32))


TQI = 256


def _dsa_index_body(qit_ref, wit_ref, kin_ref, mask_ref, key_scr, lim_scr, *, n_keep):
    i = pl.program_id(1)
    t = kin_ref.shape[0]
    nch = t // ICH
    qpos = i * TQI + _iota((1, TQI), 1)
    nact = ((i + 1) * TQI + ICH - 1) // ICH
    wi = wit_ref[...]
    zpad = jnp.zeros((LANES - DSA_IDX_DIM, TQI), BF16)
    qis = [jnp.concatenate([qit_ref[h * DSA_IDX_DIM:(h + 1) * DSA_IDX_DIM, :], zpad], axis=0)
           for h in range(DSA_IDX_HEADS)]

    for c in range(nch):
        @pl.when(c < nact)
        def _(c=c):
            kc = kin_ref[c * ICH:(c + 1) * ICH, :]
            acc = jnp.zeros((ICH, TQI), F32)
            for h in range(DSA_IDX_HEADS):
                acc = acc + wi[h:h + 1, :] * jnp.maximum(_dot(kc, qis[h]), 0.0)
            kpos = c * ICH + _iota((ICH, TQI), 0)
            key_scr[c * ICH:(c + 1) * ICH, :] = _sort_key(jnp.where(kpos <= qpos, acc, NEG_INF))

    kf = float(n_keep)
    count = _counter(key_scr, nact, 0)
    thr = _kth_largest_key(count, (1, TQI), kf)
    n_gt = count(lambda kc, idx: kc > thr)
    n_eq = count(lambda kc, idx: kc == thr)
    need = kf - n_gt
    lim_scr[...] = jnp.full(lim_scr.shape, t, I32)
    has_tie = jnp.max(jnp.where((n_eq > need) & (thr > NEG_KEY), 1.0, 0.0)) > 0.5

    @pl.when(has_tie)
    def _():
        lim_scr[...] = jnp.broadcast_to(_tie_limit(count, (1, TQI), thr, need, max(1, (t - 1).bit_length())),
                                        lim_scr.shape)

    lim = lim_scr[0:1, :]
    for c in range(nch):
        @pl.when(c < nact)
        def _(c=c):
            kc = key_scr[c * ICH:(c + 1) * ICH, :]
            kpos = c * ICH + _iota((ICH, TQI), 0)
            tie_ok = jnp.where(kc == thr, jnp.where(kpos <= lim, 1, 0), 0)
            sel = jnp.where(kc > thr, 1, tie_ok)
            mask_ref[c * ICH:(c + 1) * ICH, :] = jnp.where(kpos <= qpos, sel, 0).astype(jnp.int8)

        @pl.when(c >= nact)
        def _(c=c):
            mask_ref[c * ICH:(c + 1) * ICH, :] = jnp.zeros((ICH, TQI), jnp.int8)


def _dsa_index(qit, wit, kin, n_keep):
    b, _, t = qit.shape
    return pl.pallas_call(
        functools.partial(_dsa_index_body, n_keep=n_keep),
        out_shape=jax.ShapeDtypeStruct((b, t, t), jnp.int8),
        grid=(b, t // TQI),
        in_specs=[pl.BlockSpec((None, DSA_NQI, TQI), lambda bi, i: (bi, 0, i)),
                  pl.BlockSpec((None, DSA_IDX_HEADS, TQI), lambda bi, i: (bi, 0, i)),
                  pl.BlockSpec((None, t, LANES), lambda bi, i: (bi, 0, 0))],
        out_specs=pl.BlockSpec((None, t, TQI), lambda bi, i: (bi, 0, i)),
        scratch_shapes=[pltpu.VMEM((t, TQI), I32), pltpu.VMEM((SUBLANES, TQI), I32)],
        compiler_params=_cp("arbitrary", "arbitrary"),
        name="dsa_index",
    )(qit, wit, kin)


def _dil_body(q_ref, kp_ref, kc_ref, vp_ref, vc_ref, o_ref, lse_ref):
    i = pl.program_id(2)
    n = DIL_BLOCK
    lane = _iota((n, LANES), 1)
    lo = lane < HEAD_DIM
    row = _iota((n, n), 0)
    col = _iota((n, n), 1)
    ok_cur = row >= col
    ok_prev = (col >= row) & (i > 0)
    for hp in range(HD // LANES):
        sl = slice(hp * LANES, (hp + 1) * LANES)
        q = q_ref[:, sl].astype(F32)
        kc = kc_ref[:, sl].astype(BF16)
        kp = kp_ref[:, sl].astype(BF16)
        vc = vc_ref[:, sl].astype(BF16)
        vp = vp_ref[:, sl].astype(BF16)
        outs, lses = [], []
        for h in range(2):
            qh = (jnp.where(lo, q, 0.0) if h == 0 else jnp.where(lo, 0.0, q)).astype(BF16)
            sc = jnp.where(ok_cur, _dot_nt(qh, kc), NEG_INF)
            sp = jnp.where(ok_prev, _dot_nt(qh, kp), NEG_INF)
            m = jnp.maximum(jnp.max(sc, axis=1, keepdims=True), jnp.max(sp, axis=1, keepdims=True))
            pc = jnp.exp(sc - m)
            pp = jnp.exp(sp - m)
            l = jnp.sum(pc, axis=1, keepdims=True) + jnp.sum(pp, axis=1, keepdims=True)
            outs.append((_dot(pc.astype(BF16), vc) + _dot(pp.astype(BF16), vp)) / l)
            lses.append(m + jnp.log(l))
        o_ref[:, sl] = jnp.where(lo, outs[0], outs[1])
        lse_ref[:, sl] = jnp.where(lo, lses[0], lses[1])


def _dil_band(q, kv, dil):
    b, t, _ = q.shape
    ls = t // dil
    nbk = ls // DIL_BLOCK
    qv = q.reshape(b, ls, dil * HD)
    kvv = kv.reshape(b, ls, dil * 2 * HD)
    blk = lambda f: pl.BlockSpec((None, DIL_BLOCK, HD), f)
    cur = lambda off: (lambda bi, r, i: (bi, i, 2 * r + off))
    prev = lambda off: (lambda bi, r, i: (bi, jnp.maximum(i - 1, 0), 2 * r + off))
    o, lse = pl.pallas_call(
        _dil_body,
        out_shape=[jax.ShapeDtypeStruct((b, ls, dil * HD), F32)] * 2,
        grid=(b, dil, nbk),
        in_specs=[blk(lambda bi, r, i: (bi, i, r)), blk(prev(0)), blk(cur(0)), blk(prev(1)), blk(cur(1))],
        out_specs=[blk(lambda bi, r, i: (bi, i, r))] * 2,
        compiler_params=_cp("arbitrary", "arbitrary", "arbitrary"),
        name="dil_band",
    )(qv, kvv, kvv, kvv, kvv)
    return o.reshape(b, t, HD), lse.reshape(b, t, HD)


APG = 4


def _heads3(x):
    return x.reshape(x.shape[0] // HEAD_DIM, HEAD_DIM, x.shape[1])


def _paged_attn_body(pt_ref, q_ref, *refs):
    pages = refs[:APG]
    bias_ref, kvn_ref, bn_ref, o_ref, m_scr, l_scr, acc_scr = refs[APG:]
    s_idx = pl.program_id(1)
    q3 = _heads3(q_ref[...])
    lane0 = _iota((N_HEADS, LANES), 1) == 0

    @pl.when(s_idx == 0)
    def _():
        s_new = jnp.sum(q3 * _heads3(kvn_ref[0:HD, :]), axis=1) + bn_ref[...]
        m_scr[...] = s_new
        l_scr[...] = jnp.where(lane0, 1.0, 0.0)
        acc_scr[...] = jnp.where(_iota((HD, LANES), 1) == 0, kvn_ref[HD:2 * HD, :], 0.0)

    m = m_scr[:, 0:1]
    l = l_scr[...]
    acc = _heads3(acc_scr[...])
    for p in range(APG):
        pg = pages[p]
        s = jnp.sum(_heads3(pg[0:HD, :]) * q3, axis=1) + bias_ref[:, p * PAGE_SIZE:(p + 1) * PAGE_SIZE]
        mn = jnp.maximum(m, jnp.max(s, axis=1, keepdims=True))
        alpha = jnp.exp(m - mn)
        pr = jnp.exp(s - mn)
        l = alpha * l + pr
        acc = alpha[:, :, None] * acc + pr[:, None, :] * _heads3(pg[HD:2 * HD, :])
        m = mn
    m_scr[...] = jnp.broadcast_to(m, m_scr.shape)
    l_scr[...] = l
    acc_scr[...] = acc.reshape(HD, LANES)

    @pl.when(s_idx == pl.num_programs(1) - 1)
    def _():
        lsum = jnp.sum(l, axis=1, keepdims=True)
        o3 = jnp.sum(acc, axis=2, keepdims=True) / lsum[:, :, None]
        o_ref[...] = jnp.broadcast_to(o3, (N_HEADS, HEAD_DIM, LANES)).reshape(HD, LANES)


def _paged_attn(page_table, q_rep, cache_t, bias, kvn_rep, bias_new):
    bs, n_pages = page_table.shape
    bh = bias.shape[1]
    page_spec = lambda p: pl.BlockSpec(
        (None, 2 * HD, PAGE_SIZE), lambda bi, s, pt: (pt[bi * n_pages + s * APG + p], 0, 0))
    grid_spec = pltpu.PrefetchScalarGridSpec(
        num_scalar_prefetch=1,
        grid=(bs, n_pages // APG),
        in_specs=[pl.BlockSpec((None, HD, LANES), lambda bi, s, pt: (bi, 0, 0))]
        + [page_spec(p) for p in range(APG)]
        + [pl.BlockSpec((None, bh, APG * PAGE_SIZE), lambda bi, s, pt: (bi, 0, s)),
           pl.BlockSpec((None, 2 * HD, LANES), lambda bi, s, pt: (bi, 0, 0)),
           pl.BlockSpec((None, bh, LANES), lambda bi, s, pt: (bi, 0, 0))],
        out_specs=pl.BlockSpec((None, HD, LANES), lambda bi, s, pt: (bi, 0, 0)),
        scratch_shapes=[pltpu.VMEM((N_HEADS, LANES), F32), pltpu.VMEM((N_HEADS, LANES), F32),
                        pltpu.VMEM((HD, LANES), F32)],
    )
    return pl.pallas_call(
        _paged_attn_body,
        out_shape=jax.ShapeDtypeStruct((bs, HD, LANES), F32),
        grid_spec=grid_spec,
        compiler_params=_cp("arbitrary", "arbitrary"),
        name="paged_attn",
    )(page_table.reshape(-1), q_rep, *([cache_t] * APG), bias, kvn_rep, bias_new)


def _top3_lanes(g, blk):
    idx = []
    for _ in range(MOBA_TOPK):
        m = jnp.max(g, axis=1, keepdims=True)
        ix = jnp.min(jnp.where(g == m, blk, 1e9), axis=1, keepdims=True)
        idx.append(ix)
        g = jnp.where(blk == ix, -3e38, g)
    return idx


def _moba_gate_body(pt_ref, q_ref, p0_ref, p1_ref, bias_ref, gate_scr):
    n = pl.program_id(1)
    nb = pl.num_programs(1)
    past = bias_ref.shape[1]
    sd = jnp.sum(_heads3(p0_ref[...] + p1_ref[...]) * _heads3(q_ref[...]), axis=1)
    g = jnp.sum(sd, axis=1, keepdims=True) * (1.0 / MOBA_BLOCK)
    lane = _iota((N_HEADS, LANES), 1)

    @pl.when(n == 0)
    def _():
        gate_scr[...] = jnp.full(gate_scr.shape, NEG_INF, F32)

    gate_scr[...] = jnp.where(lane == n, g, gate_scr[...])

    @pl.when(n == nb - 1)
    def _():
        i1, i2, i3 = _top3_lanes(gate_scr[...], lane.astype(F32))
        blk = (_iota((N_HEADS, past), 1) // MOBA_BLOCK).astype(F32)
        bias_ref[...] = jnp.where((blk == i1) | (blk == i2) | (blk == i3), 0.0, NEG_INF)


def _moba_gate(page_table, q_rep, cache_t):
    bs, n_pages = page_table.shape
    ppb = MOBA_BLOCK // PAGE_SIZE
    past = n_pages * PAGE_SIZE
    page_spec = lambda p: pl.BlockSpec(
        (None, HD, PAGE_SIZE), lambda bi, n, pt: (pt[bi * n_pages + n * ppb + p], 0, 0))
    grid_spec = pltpu.PrefetchScalarGridSpec(
        num_scalar_prefetch=1,
        grid=(bs, n_pages // ppb),
        in_specs=[pl.BlockSpec((None, HD, LANES), lambda bi, n, pt: (bi, 0, 0)), page_spec(0), page_spec(1)],
        out_specs=pl.BlockSpec((None, N_HEADS, past), lambda bi, n, pt: (bi, 0, 0)),
        scratch_shapes=[pltpu.VMEM((N_HEADS, LANES), F32)],
    )
    return pl.pallas_call(
        _moba_gate_body,
        out_shape=jax.ShapeDtypeStruct((bs, N_HEADS, past), F32),
        grid_spec=grid_spec,
        compiler_params=_cp("arbitrary", "arbitrary"),
        name="moba_gate",
    )(page_table.reshape(-1), q_rep, cache_t, cache_t)


SPG = 8


def _dsa_score_body(pt_ref, qi_ref, wi_ref, *refs):
    pages = refs[:SPG]
    out_ref = refs[SPG]
    qi = qi_ref[...]
    wi = wi_ref[...]
    for p in range(SPG):
        d = _dot(qi, pages[p][...].astype(BF16))
        out_ref[:, p * PAGE_SIZE:(p + 1) * PAGE_SIZE] = jnp.sum(wi * jnp.maximum(d, 0.0), axis=0, keepdims=True)


def _dsa_scores(page_table, qi, wi, kidx_t):
    bs, n_pages = page_table.shape
    page_spec = lambda p: pl.BlockSpec(
        (None, DSA_IDX_DIM, PAGE_SIZE), lambda bi, s, pt: (pt[bi * n_pages + s * SPG + p], 0, 0))
    grid_spec = pltpu.PrefetchScalarGridSpec(
        num_scalar_prefetch=1,
        grid=(bs, n_pages // SPG),
        in_specs=[pl.BlockSpec((None, DSA_IDX_HEADS, DSA_IDX_DIM), lambda bi, s, pt: (bi, 0, 0)),
                  pl.BlockSpec((None, DSA_IDX_HEADS, 1), lambda bi, s, pt: (bi, 0, 0))]
        + [page_spec(p) for p in range(SPG)],
        out_specs=pl.BlockSpec((None, 1, SPG * PAGE_SIZE), lambda bi, s, pt: (bi, 0, s)),
    )
    return pl.pallas_call(
        _dsa_score_body,
        out_shape=jax.ShapeDtypeStruct((bs, 1, n_pages * PAGE_SIZE), F32),
        grid_spec=grid_spec,
        compiler_params=_cp("arbitrary", "arbitrary"),
        name="dsa_scores",
    )(page_table.reshape(-1), qi, wi, *([kidx_t] * SPG))


def _dsa_select_body(sc_ref, qi_ref, kiwi_ref, bias_ref, bn_ref, key_scr, lim_scr, *, n_keep):
    bs, past = sc_ref.shape
    nch = past // ICH
    for c in range(nch):
        key_scr[:, c * ICH:(c + 1) * ICH] = _sort_key(sc_ref[:, c * ICH:(c + 1) * ICH])
    kiwi = kiwi_ref[...]
    qi = qi_ref[...].astype(F32)
    s_new = jnp.zeros((bs, 1), F32)
    for h in range(DSA_IDX_HEADS):
        d = jnp.sum(qi[:, h * DSA_IDX_DIM:(h + 1) * DSA_IDX_DIM] * kiwi[:, 0:DSA_IDX_DIM],
                    axis=1, keepdims=True)
        s_new = s_new + kiwi[:, DSA_IDX_DIM + h:DSA_IDX_DIM + h + 1] * jnp.maximum(d, 0.0)
    key_new = _sort_key(s_new)

    kf = float(n_keep)
    count = _counter(key_scr, nch, 1)
    thr = _kth_largest_key(count, (bs, 1), kf, extra_key=key_new)
    n_gt = count(lambda kc, idx: kc > thr) + jnp.where(key_new > thr, 1.0, 0.0)
    n_eq = count(lambda kc, idx: kc == thr)
    need = kf - n_gt
    lim_scr[...] = jnp.full(lim_scr.shape, past, I32)
    has_tie = jnp.max(jnp.where(n_eq > need, 1.0, 0.0)) > 0.5

    @pl.when(has_tie)
    def _():
        lim_scr[...] = jnp.broadcast_to(_tie_limit(count, (bs, 1), thr, need, past.bit_length()), lim_scr.shape)

    lim = lim_scr[:, 0:1]
    for c in range(nch):
        kc = key_scr[:, c * ICH:(c + 1) * ICH]
        kpos = c * ICH + _iota((bs, ICH), 1)
        tie_ok = jnp.where(kc == thr, jnp.where(kpos <= lim, 0.0, NEG_INF), NEG_INF)
        bias_ref[:, c * ICH:(c + 1) * ICH] = jnp.where(kc > thr, 0.0, tie_ok)
    new_ok = (key_new > thr) | ((key_new == thr) & (n_eq < need))
    bn_ref[...] = jnp.broadcast_to(jnp.where(new_ok, 0.0, NEG_INF), bn_ref.shape)


def _dsa_select(scores, qi, kiwi, n_keep):
    bs, past = scores.shape
    return pl.pallas_call(
        functools.partial(_dsa_select_body, n_keep=n_keep),
        out_shape=[jax.ShapeDtypeStruct((bs, past), F32), jax.ShapeDtypeStruct((bs, LANES), F32)],
        scratch_shapes=[pltpu.VMEM((bs, past), I32), pltpu.VMEM((bs, LANES), I32)],
        compiler_params=pltpu.CompilerParams(vmem_limit_bytes=VMEM_LIMIT),
        name="dsa_select",
    )(scores, qi, kiwi)


FPG = 8


def _fox_bias_body(pt_ref, lfn_ref, *refs):
    pages = refs[:FPG]
    bias_ref, carry_scr = refs[FPG:]

    @pl.when(pl.program_id(1) == 0)
    def _():
        carry_scr[...] = lfn_ref[...]

    later = (_iota((PAGE_SIZE, PAGE_SIZE), 0) > _iota((PAGE_SIZE, PAGE_SIZE), 1)).astype(F32)
    carry = carry_scr[...]
    for p in range(FPG - 1, -1, -1):
        x = pages[p][...]
        bias_ref[:, p * PAGE_SIZE:(p + 1) * PAGE_SIZE] = _dot(x, later, precision=HIGHEST) + carry
        carry = carry + jnp.sum(x, axis=1, keepdims=True)
    carry_scr[...] = carry


def _fox_bias(page_table, logf_t, lf_new):
    bs, n_pages = page_table.shape
    nst = n_pages // FPG
    page_spec = lambda p: pl.BlockSpec(
        (None, N_HEADS, PAGE_SIZE),
        lambda bi, s, pt: (pt[bi * n_pages + (nst - 1 - s) * FPG + p], 0, 0))
    grid_spec = pltpu.PrefetchScalarGridSpec(
        num_scalar_prefetch=1,
        grid=(bs, nst),
        in_specs=[pl.BlockSpec((None, N_HEADS, LANES), lambda bi, s, pt: (bi, 0, 0))]
        + [page_spec(p) for p in range(FPG)],
        out_specs=pl.BlockSpec((None, N_HEADS, FPG * PAGE_SIZE), lambda bi, s, pt: (bi, 0, nst - 1 - s)),
        scratch_shapes=[pltpu.VMEM((N_HEADS, LANES), F32)],
    )
    return pl.pallas_call(
        _fox_bias_body,
        out_shape=jax.ShapeDtypeStruct((bs, N_HEADS, n_pages * PAGE_SIZE), F32),
        grid_spec=grid_spec,
        compiler_params=_cp("arbitrary", "arbitrary"),
        name="fox_bias",
    )(page_table.reshape(-1), lf_new, *([logf_t] * FPG))


def _dil_sample_body(*refs):
    n_grp = len(DIL_WINDOWS)
    q_refs = refs[:n_grp]
    kvn_refs = refs[n_grp:2 * n_grp]
    buf_refs = refs[2 * n_grp:3 * n_grp]
    y_ref = refs[3 * n_grp]
    win_refs = refs[3 * n_grp + 1:]
    outs, lses = [], []
    for g in range(n_grp):
        dil = DIL_DILATIONS[g]
        w = buf_refs[g].shape[2]
        q3 = _heads3(q_refs[g][...])
        kn = kvn_refs[g][0]
        vn = kvn_refs[g][1]
        kb = buf_refs[g][0]
        vb = buf_refs[g][1]
        nt = w // LANES
        s_new = jnp.sum(q3 * _heads3(kn), axis=1)[:, 0:1]
        ss = [jnp.sum(_heads3(kb[:, c * LANES:(c + 1) * LANES]) * q3, axis=1) for c in range(nt)]
        lane = _iota((2, LANES), 1)
        ss = [jnp.where(((c * LANES + lane) % dil) == 0, s, NEG_INF) for c, s in enumerate(ss)]
        m = functools.reduce(jnp.maximum, [jnp.max(s, axis=1, keepdims=True) for s in ss] + [s_new])
        ps = [jnp.exp(s - m) for s in ss]
        p_new = jnp.exp(s_new - m)
        l = functools.reduce(lambda a, b: a + b, [jnp.sum(p, axis=1, keepdims=True) for p in ps]) + p_new
        acc = functools.reduce(lambda a, b: a + b,
                               [p[:, None, :] * _heads3(vb[:, c * LANES:(c + 1) * LANES]) for c, p in enumerate(ps)])
        o = jnp.sum(acc, axis=2, keepdims=True) + p_new[:, :, None] * _heads3(vn)[:, :, 0:1]
        outs.append(o / l[:, :, None])
        lses.append(m + jnp.log(l))
        last = _iota((LANES, w), 1) == w - 1
        win_refs[g][0] = jnp.where(last, kn[:, 0:1], pltpu.roll(kb, w - 1, 1))
        win_refs[g][1] = jnp.where(last, vn[:, 0:1], pltpu.roll(vb, w - 1, 1))
    m = functools.reduce(jnp.maximum, lses)
    es = [jnp.exp(l - m) for l in lses]
    den = functools.reduce(lambda a, b: a + b, es)
    y = functools.reduce(lambda a, b: a + b, [(e / den)[:, :, None] * o for e, o in zip(es, outs)])
    y_ref[...] = jnp.broadcast_to(y, (2, HEAD_DIM, LANES)).reshape(LANES, LANES)


def _dil_sample(q_reps, kvn_reps, bufs_t):
    bs = q_reps[0].shape[0]
    nhp = HD // LANES
    args, specs = [], []
    for q in q_reps:
        args.append(q)
        specs.append(pl.BlockSpec((None, LANES, LANES), lambda bi, hp: (bi, hp, 0)))
    for kvn in kvn_reps:
        args.append(kvn)
        specs.append(pl.BlockSpec((None, 2, LANES, LANES), lambda bi, hp: (bi, 0, hp, 0)))
    win_specs, win_shapes = [], []
    for buf in bufs_t:
        w = buf.shape[3]
        args.append(buf)
        spec = pl.BlockSpec((None, 2, LANES, w), lambda bi, hp: (bi, 0, hp, 0))
        specs.append(spec)
        win_specs.append(spec)
        win_shapes.append(jax.ShapeDtypeStruct(buf.shape, F32))
    return pl.pallas_call(
        _dil_sample_body,
        out_shape=[jax.ShapeDtypeStruct((bs, HD, LANES), F32)] + win_shapes,
        grid=(bs, nhp),
        in_specs=specs,
        out_specs=[pl.BlockSpec((None, LANES, LANES), lambda bi, hp: (bi, hp, 0))] + win_specs,
        compiler_params=_cp("arbitrary", "arbitrary"),
        name="dil_sample",
    )(*args)


def _rope_angles(pos):
    inv = ROPE_THETA ** (-jnp.arange(HALF, dtype=F32) / HALF)
    ang = pos.astype(F32)[:, None] * inv
    return jnp.cos(ang), jnp.sin(ang)


def _rope_tables(pos):
    cos, sin = _rope_angles(pos)
    return (jnp.concatenate([cos, cos, cos, cos], -1), jnp.concatenate([-sin, sin, -sin, sin], -1))


def _pad_cols(w, n):
    return jnp.pad(w, ((0, 0), (0, n - w.shape[1])))


def _kv5(kv):
    return kv.reshape(kv.shape[0], kv.shape[1], 2, N_HEADS, HEAD_DIM)


def _kv5_t(kvt):
    b, _, t = kvt.shape
    return kvt.reshape(b, 2, N_HEADS, HEAD_DIM, t).transpose(0, 4, 1, 2, 3)


def _chan_major(x5):
    n, rows = x5.shape[:2]
    return x5.transpose(0, 2, 3, 4, 1).reshape(n, 2 * HD, rows)


def _lane_rep(x):
    return jnp.broadcast_to(x.astype(F32)[..., None], x.shape + (LANES,))


def kernel(x_prompt, x_sample, c_prompt, c_sample, cache_kv_moba, cache_kv_dsa, cache_kidx_dsa, cache_kv_fox, cache_logf_fox, state_win_d1, state_win_d4, state_win_d16, page_table, w_ada, b_ada, ln_g, ln_b, ffn_w_up, ffn_w_down, moba_w_qkv, moba_w_o, dsa_w_in, dsa_w_o, fox_w_in, fox_b_f, fox_w_o, dil_w_qkv, dil_w_o):
    bp, seq, d = x_prompt.shape
    bs = x_sample.shape[0]
    past = page_table.shape[1] * PAGE_SIZE
    depth = w_ada.shape[0]

    mp = -(-(bp + bs) // SUBLANES) * SUBLANES
    c_all = jnp.pad(jnp.concatenate([c_prompt, c_sample], 0), ((0, mp - bp - bs), (0, 0)))
    ada_all = _ada_all(c_all, w_ada, b_ada)
    ada_p = ada_all[:, :bp].reshape(depth, bp, N_ADA, 1, d)
    ada_s = ada_all[:, bp:bp + bs]

    pos_p = jnp.arange(seq, dtype=I32)
    cos_p, sin_p = _rope_tables(pos_p)
    cos_a, sin_a = _rope_angles(pos_p)
    tabs_p = (cos_p, sin_p, cos_a.T, sin_a.T)
    tabs_id = (jnp.ones((seq, LANES), F32), jnp.zeros((seq, LANES), F32),
               jnp.ones((HALF, seq), F32), jnp.zeros((HALF, seq), F32))
    cos_s, sin_s = _rope_tables(jnp.full((1,), past, I32))
    one_c, zero_s = jnp.ones((1, LANES), F32), jnp.zeros((1, LANES), F32)

    w_up = ffn_w_up.astype(BF16)
    w_down = ffn_w_down.astype(BF16)
    w_moba = moba_w_qkv.astype(BF16)
    w_dsa = _pad_cols(dsa_w_in, 3 * HD + DSA_NQI + LANES).astype(BF16)
    w_fox = _pad_cols(fox_w_in, 3 * HD + LANES).astype(BF16)
    w_dil = dil_w_qkv.astype(BF16)
    b_fox = jnp.pad(fox_b_f, (0, LANES - N_HEADS)).reshape(1, LANES)
    w_outs = [w.astype(BF16) for w in (moba_w_o, dsa_w_o, fox_w_o, dil_w_o)]
    zero_bn = jnp.zeros((bs, N_HEADS, LANES), F32)

    xp = x_prompt
    xs = x_sample.reshape(1, bs, d)
    res = {}
    for i in range(depth):
        ap, as_ = ada_p[i], ada_s[i]
        xp = _ffn(xp, ap, 0, w_up[i, 0], w_down[i, 0], ln_g[i, 0], ln_b[i, 0], False)
        xs = _ffn(xs, as_, 0, w_up[i, 0], w_down[i, 0], ln_g[i, 0], ln_b[i, 0], True)
        kind = i % 4
        lp = ()
        if kind == 0:
            qt, kn, kvt, kmean8 = _projt(xp, ap, moba_w_qkv, tabs_p, True, "moba")
            kmean = kmean8[:, :, :PROJT_TM // MOBA_BLOCK].reshape(bp, seq // MOBA_BLOCK, HD)
            op = [_flasht(qt, kn, kvt, "moba", (kmean,))]
            qs, kvs = _proj(xs, as_, w_moba, cos_s, sin_s, True, True)
            cache_t = _chan_major(cache_kv_moba)
            q_rep = _lane_rep(qs[0])
            bias = _moba_gate(page_table, q_rep, cache_t)
            o_rep = _paged_attn(page_table, q_rep, cache_t, bias, _lane_rep(kvs[0]), zero_bn)
            res["kv_moba"] = (_kv5_t(kvt), kvs)
        elif kind == 1:
            qt, kn, kvt, qit, kin, kit, wit = _projt(xp, ap, dsa_w_in, tabs_p, True, "dsa")
            mask = _dsa_index(qit, wit, kin, min(DSA_TOPK_MAX, seq // 4))
            op = [_flasht(qt, kn, kvt, "dsa", (mask,))]
            qs, kvs, qis, kiwis = _proj(xs, as_, w_dsa, cos_s, sin_s, True, True, "dsa")
            wi = kiwis[0, :, DSA_IDX_DIM:DSA_IDX_DIM + DSA_IDX_HEADS].reshape(bs, DSA_IDX_HEADS, 1)
            scores = _dsa_scores(page_table, qis[0].reshape(bs, DSA_IDX_HEADS, DSA_IDX_DIM), wi,
                                 cache_kidx_dsa.transpose(0, 2, 1))
            bias, bias_new = _dsa_select(scores.reshape(bs, past), qis[0], kiwis[0],
                                         min(DSA_TOPK_MAX, (past + 1) // 4))
            o_rep = _paged_attn(page_table, _lane_rep(qs[0]), _chan_major(cache_kv_dsa), bias.reshape(bs, 1, past),
                                _lane_rep(kvs[0]), bias_new.reshape(bs, 1, LANES))
            res["kv_dsa"] = (_kv5_t(kvt), kvs)
            res["kidx_dsa"] = (kit.transpose(0, 2, 1), kiwis[..., :DSA_IDX_DIM].reshape(bs, 1, DSA_IDX_DIM))
        elif kind == 2:
            qt, kn, kvt, lft, ft, fn = _projt(xp, ap, fox_w_in, tabs_id, False, "fox", fox_b_f)
            op = [_flasht(qt, kn, kvt, "fox", (fn, ft))]
            qs, kvs, lfs = _proj(xs, as_, w_fox, one_c, zero_s, True, False, "fox", b_fox)
            lf_new = jnp.broadcast_to(lfs[0, :, :N_HEADS, None], (bs, N_HEADS, LANES))
            bias = _fox_bias(page_table, cache_logf_fox.transpose(0, 2, 1), lf_new)
            o_rep = _paged_attn(page_table, _lane_rep(qs[0]), _chan_major(cache_kv_fox), bias,
                                _lane_rep(kvs[0]), zero_bn)
            res["kv_fox"] = (_kv5_t(kvt), kvs)
            res["logf_fox"] = (lft.transpose(0, 2, 1), lfs[..., :N_HEADS].reshape(bs, 1, N_HEADS))
        else:
            op, lp, q_reps, kvn_reps, bufs_t = [], [], [], [], []
            bufs = (state_win_d1, state_win_d4, state_win_d16)
            for g, (win, dil) in enumerate(zip(DIL_WINDOWS, DIL_DILATIONS)):
                wg = w_dil[:, g * 3 * HD:(g + 1) * 3 * HD]
                qp, kvp = _proj(xp, ap, wg, cos_p, sin_p, False, True)
                o, lse = _dil_band(qp, kvp, dil)
                op.append(o)
                lp.append(lse)
                qs, kvs = _proj(xs, as_, wg, cos_s, sin_s, True, True)
                q_reps.append(_lane_rep(qs[0]))
                kvn_reps.append(_lane_rep(kvs[0]).reshape(bs, 2, HD, LANES))
                bufs_t.append(_chan_major(bufs[g]).reshape(bs, 2, HD, bufs[g].shape[1]))
                res["win_%d" % g] = [_kv5(kvp[:, seq - min(win, seq):])]
            o_rep, *wins = _dil_sample(q_reps, kvn_reps, bufs_t)
            for g, wn in enumerate(wins):
                res["win_%d" % g].append(_kv5_t(wn.reshape(bs, 2 * HD, wn.shape[3])))
        os_ = [o_rep[:, :, 0].reshape(1, bs, HD)]
        xp = _oproj(xp, ap, op, lp, w_outs[kind], ln_g[i, 1], ln_b[i, 1], False)
        xs = _oproj(xs, as_, os_, (), w_outs[kind], ln_g[i, 1], ln_b[i, 1], True)
        xp = _ffn(xp, ap, 2, w_up[i, 1], w_down[i, 1], ln_g[i, 2], ln_b[i, 2], False)
        xs = _ffn(xs, as_, 2, w_up[i, 1], w_down[i, 1], ln_g[i, 2], ln_b[i, 2], True)

    kv_out = lambda name: (res[name][0], _kv5(res[name][1].reshape(bs, 1, 2 * HD)))
    return (xp, xs.reshape(bs, 1, d),
            *kv_out("kv_moba"), *kv_out("kv_dsa"), *res["kidx_dsa"],
            *kv_out("kv_fox"), *res["logf_fox"],
            *res["win_0"], *res["win_1"], *res["win_2"])
```

```python
import functools

import jax
import jax.numpy as jnp
from jax import lax
from jax.experimental import pallas as pl
from jax.experimental.pallas import tpu as pltpu

F32 = jnp.float32
BF16 = jnp.bfloat16
I32 = jnp.int32
HIGHEST = lax.Precision.HIGHEST

D_MODEL = 1024
N_HEADS = 16
HEAD_DIM = 64
HALF = HEAD_DIM // 2
HD = N_HEADS * HEAD_DIM
D_FF = 2816
N_ADA = 9
DEPTH = 4
PAGE_SIZE = 128
FFN_RES_WEIGHT = 0.5
ROPE_THETA = 10000.0
LN_EPS = 1e-5
NEG_INF = -1e30
DN_ALPHA = (2 * DEPTH) ** 0.25
MOBA_BLOCK = 256
MOBA_TOPK = 3
DSA_TOPK_MAX = 256
DSA_IDX_HEADS = 8
DSA_IDX_DIM = 64
DSA_NQI = DSA_IDX_HEADS * DSA_IDX_DIM
DSA_W_SCALE = (DSA_IDX_HEADS ** -0.5) * (DSA_IDX_DIM ** -0.5)
DIL_WINDOWS = (128, 512, 2048)
DIL_DILATIONS = (1, 4, 16)
DIL_BLOCK = 128
QK_SCALE = HEAD_DIM ** -0.5

LANES = 128
SUBLANES = 8
VMEM_LIMIT = 56 * 2 ** 20
INT_MIN = -2 ** 31
NEG_KEY = -1900671691

_NT = (((1,), (1,)), ((), ()))


def _cp(*sem):
    return pltpu.CompilerParams(dimension_semantics=sem, vmem_limit_bytes=VMEM_LIMIT)


def _dot(a, b, **kw):
    return jnp.dot(a, b, preferred_element_type=F32, **kw)


def _dot_nt(a, b, **kw):
    return lax.dot_general(a, b, _NT, preferred_element_type=F32, **kw)


def _iota(shape, dim):
    return lax.broadcasted_iota(I32, shape, dim)


def _sort_key(x):
    x = jnp.where(x == 0.0, 0.0, x)
    bits = pltpu.bitcast(x, I32)
    return jnp.where(bits < 0, bits ^ 0x7FFFFFFF, bits)


def _log_sigmoid(z):
    return jnp.minimum(z, 0.0) - jnp.log1p(jnp.exp(-jnp.abs(z)))


def _post_ln(x, y, gate, g, b, weight):
    r = DN_ALPHA * x + (weight * (1.0 + gate)) * y
    mu = jnp.mean(r, axis=-1, keepdims=True)
    rc = r - mu
    var = jnp.mean(rc * rc, axis=-1, keepdims=True)
    return rc * lax.rsqrt(var + LN_EPS) * g + b


def _ada_operand(ada, j, k, per_row):
    if per_row:
        m = ada.shape[0]
        return ada, pl.BlockSpec((m, D_MODEL), lambda *g: (0, 3 * j + k))
    return ada, pl.BlockSpec((None, None, 1, D_MODEL), lambda *g: (g[0], 3 * j + k, 0, 0))


def _ada_body(c_ref, w_ref, b_ref, o_ref):
    o_ref[...] = _dot(c_ref[...].astype(BF16), w_ref[...].astype(BF16)) + b_ref[...]


def _ada_all(c_all, w_ada, b_ada):
    m = c_all.shape[0]
    depth, d, n = w_ada.shape
    tn = 1152
    return pl.pallas_call(
        _ada_body,
        out_shape=jax.ShapeDtypeStruct((depth, m, n), F32),
        grid=(depth, n // tn),
        in_specs=[pl.BlockSpec((m, d), lambda l, c: (0, 0)),
                  pl.BlockSpec((None, d, tn), lambda l, c: (l, 0, c)),
                  pl.BlockSpec((None, 1, tn), lambda l, c: (l, 0, c))],
        out_specs=pl.BlockSpec((None, m, tn), lambda l, c: (l, 0, c)),
        compiler_params=_cp("arbitrary", "arbitrary"),
        name="ada",
    )(c_all, w_ada, b_ada.reshape(depth, 1, n))


def _ffn_body(x_ref, sh_ref, sc_ref, gt_ref, wg_ref, wu_ref, wd_ref, lg_ref, lb_ref, o_ref,
              h_scr, acc_scr):
    f = pl.program_id(2)

    @pl.when(f == 0)
    def _():
        h_scr[...] = (x_ref[...] * (1.0 + sc_ref[...]) + sh_ref[...]).astype(BF16)

    h = h_scr[...]
    g = _dot(h, wg_ref[...])
    u = _dot(h, wu_ref[...])
    a = (g * jax.nn.sigmoid(g) * u).astype(BF16)
    y = _dot(a, wd_ref[...])

    @pl.when(f == 0)
    def _():
        acc_scr[...] = y

    @pl.when(f > 0)
    def _():
        acc_scr[...] += y

    @pl.when(f == pl.num_programs(2) - 1)
    def _():
        o_ref[...] = _post_ln(x_ref[...], acc_scr[...], gt_ref[...], lg_ref[...], lb_ref[...],
                              FFN_RES_WEIGHT)


def _ffn(x, ada, j, w_up, w_down, ln_g, ln_b, per_row):
    b, t, d = x.shape
    ff = w_down.shape[0]
    tm = min(t, 512)
    fc = 1408 if ff % 1408 == 0 else ff
    nf = ff // fc
    sh, sh_spec = _ada_operand(ada, j, 0, per_row)
    sc, sc_spec = _ada_operand(ada, j, 1, per_row)
    gt, gt_spec = _ada_operand(ada, j, 2, per_row)
    return pl.pallas_call(
        _ffn_body,
        out_shape=jax.ShapeDtypeStruct((b, t, d), F32),
        grid=(b, t // tm, nf),
        in_specs=[pl.BlockSpec((None, tm, d), lambda bi, i, f: (bi, i, 0)),
                  sh_spec, sc_spec, gt_spec,
                  pl.BlockSpec((d, fc), lambda bi, i, f: (0, f)),
                  pl.BlockSpec((d, fc), lambda bi, i, f: (0, f + nf)),
                  pl.BlockSpec((fc, d), lambda bi, i, f: (f, 0)),
                  pl.BlockSpec((1, d), lambda bi, i, f: (0, 0)),
                  pl.BlockSpec((1, d), lambda bi, i, f: (0, 0))],
        out_specs=pl.BlockSpec((None, tm, d), lambda bi, i, f: (bi, i, 0)),
        scratch_shapes=[pltpu.VMEM((tm, d), BF16), pltpu.VMEM((tm, d), F32)],
        compiler_params=_cp("arbitrary", "arbitrary", "arbitrary"),
        name="ffn",
    )(x, sh, sc, gt, w_up, w_up, w_down, ln_g.reshape(1, d), ln_b.reshape(1, d))


def _rope_lanes(zc, cos, sin, first_half):
    sw = jnp.where(first_half, pltpu.roll(zc, 96, 1), pltpu.roll(zc, 32, 1))
    return zc * cos + sw * sin


def _rope_rows_store(ref, row0, z, cos, sin, scale=None, dtype=F32):
    for h in range(z.shape[0] // HEAD_DIM):
        x1 = z[h * HEAD_DIM:h * HEAD_DIM + HALF, :]
        x2 = z[h * HEAD_DIM + HALF:(h + 1) * HEAD_DIM, :]
        o1 = x1 * cos - x2 * sin
        o2 = x2 * cos + x1 * sin
        if scale is not None:
            o1, o2 = o1 * scale, o2 * scale
        r = row0 + h * HEAD_DIM
        ref[r:r + HALF, :] = o1.astype(dtype)
        ref[r + HALF:r + HEAD_DIM, :] = o2.astype(dtype)


def _proj_body(*refs, rope, extra):
    x_ref, sh_ref, sc_ref, cos_ref, sin_ref, w_ref = refs[:6]
    rest = refs[6:]
    if extra == "fox":
        bf_ref, q_ref, kv_ref, lf_ref = rest
    elif extra == "dsa":
        q_ref, kv_ref, qi_ref, kiwi_ref = rest
    else:
        q_ref, kv_ref = rest
    tm = x_ref.shape[0]
    h = (x_ref[...] * (1.0 + sc_ref[...]) + sh_ref[...]).astype(BF16)
    cos = cos_ref[...]
    sin = sin_ref[...]
    lane = _iota((tm, LANES), 1)
    first_half = (lane & HALF) == 0

    zq = _dot(h, w_ref[:, 0:HD])
    for s in range(HD // LANES):
        sl = slice(s * LANES, (s + 1) * LANES)
        c = zq[:, sl]
        q_ref[:, sl] = ((_rope_lanes(c, cos, sin, first_half) if rope else c) * QK_SCALE).astype(q_ref.dtype)
    zk = _dot(h, w_ref[:, HD:2 * HD])
    for s in range(HD // LANES):
        sl = slice(s * LANES, (s + 1) * LANES)
        c = zk[:, sl]
        kv_ref[:, sl] = (_rope_lanes(c, cos, sin, first_half) if rope else c).astype(kv_ref.dtype)
    kv_ref[:, HD:2 * HD] = _dot(h, w_ref[:, 2 * HD:3 * HD]).astype(kv_ref.dtype)

    if extra == "dsa":
        ze = _dot(h, w_ref[:, 3 * HD:3 * HD + DSA_NQI + LANES])
        for s in range(DSA_NQI // LANES):
            sl = slice(s * LANES, (s + 1) * LANES)
            qi_ref[:, sl] = _rope_lanes(ze[:, sl], cos, sin, first_half).astype(BF16)
        c = ze[:, DSA_NQI:DSA_NQI + LANES]
        kiwi_ref[...] = jnp.where(lane < DSA_IDX_DIM, _rope_lanes(c, cos, sin, first_half), c * DSA_W_SCALE)
    elif extra == "fox":
        lf_ref[...] = _log_sigmoid(_dot(h, w_ref[:, 3 * HD:3 * HD + LANES]) + bf_ref[...])


def _proj(x, ada, w, cos, sin, per_row, rope, extra=None, b_f=None, qkv_dtype=(BF16, F32)):
    b, t, d = x.shape
    n = w.shape[1]
    tm = min(t, 512)
    sh, sh_spec = _ada_operand(ada, 1, 0, per_row)
    sc, sc_spec = _ada_operand(ada, 1, 1, per_row)
    tcs = cos.shape[0]
    cs_spec = (pl.BlockSpec((1, LANES), lambda bi, i: (0, 0)) if tcs == 1
               else pl.BlockSpec((tm, LANES), lambda bi, i: (i, 0)))
    row = lambda w_: pl.BlockSpec((None, tm, w_), lambda bi, i: (bi, i, 0))
    in_specs = [row(d), sh_spec, sc_spec, cs_spec, cs_spec,
                pl.BlockSpec((d, n), lambda bi, i: (0, 0))]
    args = [x, sh, sc, cos, sin, w]
    out_shape = [jax.ShapeDtypeStruct((b, t, HD), qkv_dtype[0]),
                 jax.ShapeDtypeStruct((b, t, 2 * HD), qkv_dtype[1])]
    out_specs = [row(HD), row(2 * HD)]
    if extra == "dsa":
        out_shape += [jax.ShapeDtypeStruct((b, t, DSA_NQI), BF16), jax.ShapeDtypeStruct((b, t, LANES), F32)]
        out_specs += [row(DSA_NQI), row(LANES)]
    elif extra == "fox":
        in_specs.append(pl.BlockSpec((1, LANES), lambda bi, i: (0, 0)))
        args.append(b_f)
        out_shape += [jax.ShapeDtypeStruct((b, t, LANES), F32)]
        out_specs += [row(LANES)]
    return pl.pallas_call(
        functools.partial(_proj_body, rope=rope, extra=extra),
        out_shape=out_shape,
        grid=(b, t // tm),
        in_specs=in_specs,
        out_specs=out_specs,
        compiler_params=_cp("arbitrary", "arbitrary"),
        name="proj_" + (extra or "qkv"),
    )(*args)


PROJT_TM = 512


def _projt_body(*refs, rope, extra):
    x_ref, sh_ref, sc_ref, cosn_ref, sinn_ref, cost_ref, sint_ref, wkn_ref, wt_ref = refs[:9]
    rest = refs[9:]
    if extra == "moba":
        qt_ref, kn_ref, kvt_ref, kmean_ref = rest
    elif extra == "dsa":
        wet_ref, wkin_ref, qt_ref, kn_ref, kvt_ref, qit_ref, kin_ref, kit_ref, wit_ref = rest
    else:
        wft_ref, wfn_ref, bft_ref, bfn_ref, qt_ref, kn_ref, kvt_ref, lft_ref, ft_ref, fn_ref, ct_scr, cn_scr = rest
    i = pl.program_id(1)
    tm = x_ref.shape[0]
    h = (x_ref[...] * (1.0 + sc_ref[...]) + sh_ref[...]).astype(BF16)
    lane = _iota((tm, LANES), 1)
    first_half = (lane & HALF) == 0
    if rope:
        cosn, sinn, cost, sint = cosn_ref[...], sinn_ref[...], cost_ref[...], sint_ref[...]

    zk = _dot(h, wkn_ref[...])
    kparts = []
    for s in range(HD // LANES):
        sl = slice(s * LANES, (s + 1) * LANES)
        c = zk[:, sl]
        c = _rope_lanes(c, cosn, sinn, first_half) if rope else c
        kn_ref[:, sl] = c.astype(BF16)
        kparts.append(c)
    if extra == "moba":
        kmean_ref[...] = jnp.zeros_like(kmean_ref)
        for nb in range(tm // MOBA_BLOCK):
            for s in range(HD // LANES):
                blk = kparts[s][nb * MOBA_BLOCK:(nb + 1) * MOBA_BLOCK, :]
                kmean_ref[nb:nb + 1, s * LANES:(s + 1) * LANES] = (
                    jnp.sum(blk, axis=0, keepdims=True) * (1.0 / MOBA_BLOCK))

    zq = _dot_nt(wt_ref[0:HD, :], h)
    if rope:
        _rope_rows_store(qt_ref, 0, zq, cost, sint, scale=QK_SCALE, dtype=BF16)
    else:
        qt_ref[...] = (zq * QK_SCALE).astype(BF16)
    zkt = _dot_nt(wt_ref[HD:2 * HD, :], h)
    if rope:
        _rope_rows_store(kvt_ref, 0, zkt, cost, sint)
    else:
        kvt_ref[0:HD, :] = zkt
    kvt_ref[HD:2 * HD, :] = _dot_nt(wt_ref[2 * HD:3 * HD, :], h)

    if extra == "dsa":
        ze = _dot_nt(wet_ref[...], h)
        _rope_rows_store(qit_ref, 0, ze[0:DSA_NQI, :], cost, sint, dtype=BF16)
        _rope_rows_store(kit_ref, 0, ze[DSA_NQI:DSA_NQI + DSA_IDX_DIM, :], cost, sint)
        wit_ref[...] = ze[DSA_NQI + DSA_IDX_DIM:, :] * DSA_W_SCALE
        zkin = _dot(h, wkin_ref[...])
        kin_ref[...] = jnp.where(lane < DSA_IDX_DIM, _rope_lanes(zkin, cosn, sinn, first_half), 0.0).astype(BF16)
    elif extra == "fox":
        @pl.when(i == 0)
        def _():
            ct_scr[...] = jnp.zeros_like(ct_scr)
            cn_scr[...] = jnp.zeros_like(cn_scr)

        lft = _log_sigmoid(_dot_nt(wft_ref[...], h) + bft_ref[...])
        lft_ref[...] = lft
        upper = (_iota((tm, tm), 0) <= _iota((tm, tm), 1)).astype(F32)
        ft = _dot(lft, upper, precision=HIGHEST) + ct_scr[:, 0:1]
        ft_ref[...] = ft
        ct_scr[...] = jnp.broadcast_to(ft[:, tm - 1:tm], ct_scr.shape)
        lfn = _log_sigmoid(_dot(h, wfn_ref[...]) + bfn_ref[...])
        lower = (_iota((tm, tm), 0) >= _iota((tm, tm), 1)).astype(F32)
        fn = _dot(lower, lfn, precision=HIGHEST) + cn_scr[...]
        fn_ref[...] = fn
        cn_scr[...] = fn[tm - 1:tm, :]


def _projt(x, ada, w, rope_tabs, rope, extra, b_f=None):
    b, t, d = x.shape
    tm = PROJT_TM
    cosn, sinn, cost, sint = rope_tabs
    sh, sh_spec = _ada_operand(ada, 1, 0, False)
    sc, sc_spec = _ada_operand(ada, 1, 1, False)
    wb = w.astype(BF16)
    wkn = wb[:, HD:2 * HD]
    wt = wb[:, 0:3 * HD].T
    full = lambda a: pl.BlockSpec(a.shape, lambda bi, i: (0,) * a.ndim)
    tok = lambda w_: pl.BlockSpec((None, tm, w_), lambda bi, i: (bi, i, 0))
    chan = lambda c_: pl.BlockSpec((None, c_, tm), lambda bi, i: (bi, 0, i))
    in_specs = [tok(d), sh_spec, sc_spec,
                pl.BlockSpec((tm, LANES), lambda bi, i: (i, 0)), pl.BlockSpec((tm, LANES), lambda bi, i: (i, 0)),
                pl.BlockSpec((HALF, tm), lambda bi, i: (0, i)), pl.BlockSpec((HALF, tm), lambda bi, i: (0, i)),
                full(wkn), full(wt)]
    args = [x, sh, sc, cosn, sinn, cost, sint, wkn, wt]
    out_shape = [jax.ShapeDtypeStruct((b, HD, t), BF16), jax.ShapeDtypeStruct((b, t, HD), BF16),
                 jax.ShapeDtypeStruct((b, 2 * HD, t), F32)]
    out_specs = [chan(HD), tok(HD), chan(2 * HD)]
    scratch = []
    if extra == "moba":
        out_shape.append(jax.ShapeDtypeStruct((b, t // tm, SUBLANES, HD), F32))
        out_specs.append(pl.BlockSpec((None, None, SUBLANES, HD), lambda bi, i: (bi, i, 0, 0)))
    elif extra == "dsa":
        wet = wb[:, 3 * HD:].T
        wkin = jnp.pad(wb[:, 3 * HD + DSA_NQI:3 * HD + DSA_NQI + DSA_IDX_DIM], ((0, 0), (0, LANES - DSA_IDX_DIM)))
        in_specs += [full(wet), full(wkin)]
        args += [wet, wkin]
        out_shape += [jax.ShapeDtypeStruct((b, DSA_NQI, t), BF16), jax.ShapeDtypeStruct((b, t, LANES), BF16),
                      jax.ShapeDtypeStruct((b, DSA_IDX_DIM, t), F32), jax.ShapeDtypeStruct((b, DSA_IDX_HEADS, t), F32)]
        out_specs += [chan(DSA_NQI), tok(LANES), chan(DSA_IDX_DIM), chan(DSA_IDX_HEADS)]
    else:
        wft = wb[:, 3 * HD:].T
        wfn = jnp.pad(wb[:, 3 * HD:], ((0, 0), (0, LANES - N_HEADS)))
        bft = jnp.broadcast_to(b_f.reshape(N_HEADS, 1), (N_HEADS, tm))
        bfn = jnp.pad(b_f, (0, LANES - N_HEADS)).reshape(1, LANES)
        in_specs += [full(wft), full(wfn), full(bft), full(bfn)]
        args += [wft, wfn, bft, bfn]
        out_shape += [jax.ShapeDtypeStruct((b, N_HEADS, t), F32), jax.ShapeDtypeStruct((b, N_HEADS, t), F32),
                      jax.ShapeDtypeStruct((b, t, LANES), F32)]
        out_specs += [chan(N_HEADS), chan(N_HEADS), tok(LANES)]
        scratch = [pltpu.VMEM((N_HEADS, LANES), F32), pltpu.VMEM((1, LANES), F32)]
    return pl.pallas_call(
        functools.partial(_projt_body, rope=rope, extra=extra),
        out_shape=out_shape,
        grid=(b, t // tm),
        in_specs=in_specs,
        out_specs=out_specs,
        scratch_shapes=scratch,
        compiler_params=_cp("arbitrary", "arbitrary"),
        name="projt_" + extra,
    )(*args)


def _oproj_body(*refs, n_grp):
    x_ref, gt_ref = refs[:2]
    o_refs = refs[2:2 + n_grp]
    l_refs = refs[2 + n_grp:2 + 2 * n_grp] if n_grp > 1 else ()
    w_ref, lg_ref, lb_ref, out_ref = refs[2 + len(o_refs) + len(l_refs):]
    if n_grp == 1:
        a = o_refs[0][...].astype(BF16)
    else:
        ls = [r[...] for r in l_refs]
        m = functools.reduce(jnp.maximum, ls)
        es = [jnp.exp(l - m) for l in ls]
        den = functools.reduce(lambda p, q: p + q, es)
        a = functools.reduce(lambda p, q: p + q, [(e / den) * r[...] for e, r in zip(es, o_refs)])
        a = a.astype(BF16)
    y = _dot(a, w_ref[...])
    out_ref[...] = _post_ln(x_ref[...], y, gt_ref[...], lg_ref[...], lb_ref[...], 1.0)


def _oproj(x, ada, outs, lses, w_o, ln_g, ln_b, per_row):
    b, t, d = x.shape
    tm = min(t, 512)
    gt, gt_spec = _ada_operand(ada, 1, 2, per_row)
    row = pl.BlockSpec((None, tm, d), lambda bi, i: (bi, i, 0))
    one = pl.BlockSpec((1, d), lambda bi, i: (0, 0))
    n_grp = len(outs)
    return pl.pallas_call(
        functools.partial(_oproj_body, n_grp=n_grp),
        out_shape=jax.ShapeDtypeStruct((b, t, d), F32),
        grid=(b, t // tm),
        in_specs=[row, gt_spec] + [row] * (n_grp + len(lses))
        + [pl.BlockSpec((HD, d), lambda bi, i: (0, 0)), one, one],
        out_specs=row,
        compiler_params=_cp("arbitrary", "arbitrary"),
        name="oproj",
    )(x, gt, *outs, *lses, w_o, ln_g.reshape(1, d), ln_b.reshape(1, d))


TQ = 256
TKC = 512


def _top3_rows(g, blk):
    idx = []
    for _ in range(MOBA_TOPK):
        m = jnp.max(g, axis=0, keepdims=True)
        ix = jnp.min(jnp.where(g == m, blk, 1e9), axis=0, keepdims=True)
        idx.append(ix)
        g = jnp.where(blk == ix, -3e38, g)
    return idx


def _flasht_body(*refs, mode):
    if mode == "moba":
        qt_ref, k_ref, vt_ref, km_ref, o_ref, s_scr = refs
    elif mode == "dsa":
        qt_ref, k_ref, vt_ref, mask_ref, o_ref, s_scr = refs
    else:
        qt_ref, k_ref, vt_ref, fn_ref, ft_ref, o_ref, s_scr = refs
    hp = pl.program_id(1)
    j = pl.program_id(2)
    top = _iota((LANES, TQ), 0) < HEAD_DIM
    qt = qt_ref[...].astype(F32)
    qh = (jnp.where(top, qt, 0.0).astype(BF16), jnp.where(top, 0.0, qt).astype(BF16))
    nck = (j + 2) // 2
    qpos = j * TQ + _iota((TKC, TQ), 1)
    first_blk = _iota((TKC, TQ), 0) < TQ

    if mode == "moba":
        km = km_ref[...]
        blk = _iota((km.shape[0], TQ), 0).astype(F32)
        jf = j.astype(F32)
        sel_idx = []
        for h in range(2):
            qf = jnp.where(top, qt, 0.0) if h == 0 else jnp.where(top, 0.0, qt)
            g = _dot(km, qf, precision=HIGHEST)
            sel_idx.append(_top3_rows(jnp.where(blk < jf, g, NEG_INF), blk))
    elif mode == "fox":
        fq = [ft_ref[pl.ds(2 * hp + h, 1), :] for h in range(2)]
        lane_k = _iota((TKC, LANES), 1)

    def scores(c, m_run, last):
        start = pl.multiple_of(c * TKC, TKC)
        kb = k_ref[pl.ds(start, TKC), :]
        if mode == "fox":
            f_tile = fn_ref[pl.ds(start, TKC), :]
        elif mode == "dsa":
            keep = mask_ref[pl.ds(start, TKC), :].astype(F32) > 0.5
        if last:
            causal = start + _iota((TKC, TQ), 0) <= qpos
        m_new = []
        for h in range(2):
            s = _dot(kb, qh[h])
            if mode == "fox":
                fk = jnp.sum(jnp.where(lane_k == 2 * hp + h, f_tile, 0.0), axis=1, keepdims=True)
                s = (s + fq[h]) - fk
            elif mode == "dsa":
                s = jnp.where(keep, s, NEG_INF)
            else:
                i1, i2, i3 = sel_idx[h]
                hits = []
                for half in range(2):
                    nf = (2 * c + half).astype(F32)
                    hit = jnp.where((i1 == nf) | (i2 == nf) | (i3 == nf), 1.0, 0.0)
                    if last:
                        hit = jnp.maximum(hit, jnp.where(nf >= jf, 1.0, 0.0))
                    hits.append(hit)
                s = jnp.where(jnp.where(first_blk, hits[0], hits[1]) > 0.5, s, NEG_INF)
            if last:
                s = jnp.where(causal, s, NEG_INF)
            s_scr[h, pl.ds(start, TKC), :] = s
            m_new.append(jnp.maximum(m_run[h], jnp.max(s.reshape(TKC // SUBLANES, SUBLANES, TQ), axis=0)))
        return tuple(m_new)

    def paired(fn, n, init):
        def quad(i, c):
            for u in range(4):
                c = fn(4 * i + u, c)
            return c

        carry = lax.fori_loop(0, n // 4, quad, init)
        base = (n // 4) * 4
        carry = lax.cond(n % 4 >= 2, lambda c: fn(base + 1, fn(base, c)), lambda c: c, carry)
        return lax.cond(n % 2 == 1, lambda c: fn(n - 1, c), lambda c: c, carry)

    neg = jnp.full((SUBLANES, TQ), -jnp.inf, F32)
    m_run = paired(lambda c, m: scores(c, m, False), nck - 1, (neg, neg))
    m_run = scores(nck - 1, m_run, True)
    m_fin = [jnp.max(m, axis=0, keepdims=True) for m in m_run]
    topk = _iota((LANES, TKC), 0) < HEAD_DIM

    def values(c, acc):
        start = pl.multiple_of(c * TKC, TKC)
        vb = vt_ref[:, pl.ds(start, TKC)]
        out = []
        for h in range(2):
            p = jnp.exp(s_scr[h, pl.ds(start, TKC), :] - m_fin[h]).astype(BF16)
            vext = (jnp.where(topk, vb, 1.0) if h == 0 else jnp.where(topk, 1.0, vb)).astype(BF16)
            out.append(acc[h] + _dot(vext, p))
        return tuple(out)

    acc0 = jnp.zeros((LANES, TQ), F32)
    acc = paired(values, nck, (acc0, acc0))
    l0 = acc[0][HEAD_DIM:HEAD_DIM + 1, :]
    l1 = acc[1][0:1, :]
    ot = jnp.where(top, acc[0] / l0, acc[1] / l1)
    o_ref[...] = ot.T.astype(BF16)


def _flasht(qt, kn, kvt, mode, extra=()):
    b, _, t = qt.shape
    assert t % TKC == 0, "the key chunks of the last query tile must stay inside the sequence"
    nq = t // TQ
    nhp = HD // LANES
    in_specs = [pl.BlockSpec((None, LANES, TQ), lambda bi, hp, j: (bi, hp, j)),
                pl.BlockSpec((None, t, LANES), lambda bi, hp, j: (bi, 0, hp)),
                pl.BlockSpec((None, LANES, t), lambda bi, hp, j: (bi, nhp + hp, 0))]
    if mode == "moba":
        nb = extra[0].shape[1]
        in_specs.append(pl.BlockSpec((None, nb, LANES), lambda bi, hp, j: (bi, 0, hp)))
    elif mode == "dsa":
        in_specs.append(pl.BlockSpec((None, t, TQ), lambda bi, hp, j: (bi, 0, j)))
    else:
        in_specs += [pl.BlockSpec((None, t, LANES), lambda bi, hp, j: (bi, 0, 0)),
                     pl.BlockSpec((None, N_HEADS, TQ), lambda bi, hp, j: (bi, 0, j))]
    return pl.pallas_call(
        functools.partial(_flasht_body, mode=mode),
        out_shape=jax.ShapeDtypeStruct((b, t, HD), BF16),
        grid=(b, nhp, nq),
        in_specs=in_specs,
        out_specs=pl.BlockSpec((None, TQ, LANES), lambda bi, hp, j: (bi, j, hp)),
        scratch_shapes=[pltpu.VMEM((2, t, TQ), F32)],
        compiler_params=_cp("arbitrary", "arbitrary", "arbitrary"),
        name="flash_" + mode,
    )(qt, kn, kvt, *extra)


ICH = 512


def _counter(key_scr, n_chunks, key_axis):
    nq = key_scr.shape[1 - key_axis]

    def count(pred):
        def body(c, cnt):
            st = pl.multiple_of(c * ICH, ICH)
            if key_axis == 1:
                kc = key_scr[:, pl.ds(st, ICH)]
                x = jnp.where(pred(kc, st + _iota(kc.shape, 1)), 1.0, 0.0)
                for s in range(ICH // LANES):
                    cnt = cnt + x[:, s * LANES:(s + 1) * LANES]
                return cnt
            kc = key_scr[pl.ds(st, ICH), :]
            x = jnp.where(pred(kc, st + _iota(kc.shape, 0)), 1.0, 0.0)
            return cnt + jnp.sum(x.reshape(ICH // SUBLANES, SUBLANES, nq), axis=0)

        if key_axis == 1:
            cnt = lax.fori_loop(0, n_chunks, body, jnp.zeros((nq, LANES), F32))
            return jnp.sum(cnt, axis=1, keepdims=True)
        cnt = lax.fori_loop(0, n_chunks, body, jnp.zeros((SUBLANES, nq), F32))
        return jnp.sum(cnt, axis=0, keepdims=True)

    return count


def _kth_largest_key(count, qshape, k, extra_key=None):
    def count_ge(cand):
        c = count(lambda kc, idx: kc >= cand)
        if extra_key is not None:
            c = c + jnp.where(extra_key >= cand, 1.0, 0.0)
        return c

    t0 = jnp.where(count_ge(jnp.zeros(qshape, I32)) >= k, 0, INT_MIN).astype(I32)

    def bit_body(it, cur):
        cand = cur + lax.shift_left(jnp.int32(1), 30 - it)
        return jnp.where(count_ge(cand) >= k, cand, cur)

    return lax.fori_loop(0, 31, bit_body, t0)


def _tie_limit(count, qshape, thr, need, n_bits):
    def bit_body(it, cur):
        cand = cur + lax.shift_left(jnp.int32(1), n_bits - 1 - it)
        c = count(lambda kc, idx: (kc == thr) & (idx < cand))
        return jnp.where(c < need, cand, cur)

    return lax.fori_loop(0, n_bits, bit_body, jnp.zeros(qshape, I32))


TQI = 256


def _dsa_index_body(qit_ref, wit_ref, kin_ref, mask_ref, key_scr, lim_scr, *, n_keep):
    i = pl.program_id(1)
    t = kin_ref.shape[0]
    nch = t // ICH
    qpos = i * TQI + _iota((1, TQI), 1)
    nact = ((i + 1) * TQI + ICH - 1) // ICH
    wi = wit_ref[...]
    zpad = jnp.zeros((LANES - DSA_IDX_DIM, TQI), BF16)
    qis = [jnp.concatenate([qit_ref[h * DSA_IDX_DIM:(h + 1) * DSA_IDX_DIM, :], zpad], axis=0)
           for h in range(DSA_IDX_HEADS)]

    for c in range(nch):
        @pl.when(c < nact)
        def _(c=c):
            kc = kin_ref[c * ICH:(c + 1) * ICH, :]
            acc = jnp.zeros((ICH, TQI), F32)
            for h in range(DSA_IDX_HEADS):
                acc = acc + wi[h:h + 1, :] * jnp.maximum(_dot(kc, qis[h]), 0.0)
            kpos = c * ICH + _iota((ICH, TQI), 0)
            key_scr[c * ICH:(c + 1) * ICH, :] = _sort_key(jnp.where(kpos <= qpos, acc, NEG_INF))

    kf = float(n_keep)
    count = _counter(key_scr, nact, 0)
    thr = _kth_largest_key(count, (1, TQI), kf)
    n_gt = count(lambda kc, idx: kc > thr)
    n_eq = count(lambda kc, idx: kc == thr)
    need = kf - n_gt
    lim_scr[...] = jnp.full(lim_scr.shape, t, I32)
    has_tie = jnp.max(jnp.where((n_eq > need) & (thr > NEG_KEY), 1.0, 0.0)) > 0.5

    @pl.when(has_tie)
    def _():
        lim_scr[...] = jnp.broadcast_to(_tie_limit(count, (1, TQI), thr, need, max(1, (t - 1).bit_length())),
                                        lim_scr.shape)

    lim = lim_scr[0:1, :]
    for c in range(nch):
        @pl.when(c < nact)
        def _(c=c):
            kc = key_scr[c * ICH:(c + 1) * ICH, :]
            kpos = c * ICH + _iota((ICH, TQI), 0)
            tie_ok = jnp.where(kc == thr, jnp.where(kpos <= lim, 1, 0), 0)
            sel = jnp.where(kc > thr, 1, tie_ok)
            mask_ref[c * ICH:(c + 1) * ICH, :] = jnp.where(kpos <= qpos, sel, 0).astype(jnp.int8)

        @pl.when(c >= nact)
        def _(c=c):
            mask_ref[c * ICH:(c + 1) * ICH, :] = jnp.zeros((ICH, TQI), jnp.int8)


def _dsa_index(qit, wit, kin, n_keep):
    b, _, t = qit.shape
    return pl.pallas_call(
        functools.partial(_dsa_index_body, n_keep=n_keep),
        out_shape=jax.ShapeDtypeStruct((b, t, t), jnp.int8),
        grid=(b, t // TQI),
        in_specs=[pl.BlockSpec((None, DSA_NQI, TQI), lambda bi, i: (bi, 0, i)),
                  pl.BlockSpec((None, DSA_IDX_HEADS, TQI), lambda bi, i: (bi, 0, i)),
                  pl.BlockSpec((None, t, LANES), lambda bi, i: (bi, 0, 0))],
        out_specs=pl.BlockSpec((None, t, TQI), lambda bi, i: (bi, 0, i)),
        scratch_shapes=[pltpu.VMEM((t, TQI), I32), pltpu.VMEM((SUBLANES, TQI), I32)],
        compiler_params=_cp("arbitrary", "arbitrary"),
        name="dsa_index",
    )(qit, wit, kin)


def _dil_body(q_ref, kp_ref, kc_ref, vp_ref, vc_ref, o_ref, lse_ref):
    i = pl.program_id(2)
    n = DIL_BLOCK
    lane = _iota((n, LANES), 1)
    lo = lane < HEAD_DIM
    row = _iota((n, n), 0)
    col = _iota((n, n), 1)
    ok_cur = row >= col
    ok_prev = (col >= row) & (i > 0)
    for hp in range(HD // LANES):
        sl = slice(hp * LANES, (hp + 1) * LANES)
        q = q_ref[:, sl].astype(F32)
        kc = kc_ref[:, sl].astype(BF16)
        kp = kp_ref[:, sl].astype(BF16)
        vc = vc_ref[:, sl].astype(BF16)
        vp = vp_ref[:, sl].astype(BF16)
        outs, lses = [], []
        for h in range(2):
            qh = (jnp.where(lo, q, 0.0) if h == 0 else jnp.where(lo, 0.0, q)).astype(BF16)
            sc = jnp.where(ok_cur, _dot_nt(qh, kc), NEG_INF)
            sp = jnp.where(ok_prev, _dot_nt(qh, kp), NEG_INF)
            m = jnp.maximum(jnp.max(sc, axis=1, keepdims=True), jnp.max(sp, axis=1, keepdims=True))
            pc = jnp.exp(sc - m)
            pp = jnp.exp(sp - m)
            l = jnp.sum(pc, axis=1, keepdims=True) + jnp.sum(pp, axis=1, keepdims=True)
            outs.append((_dot(pc.astype(BF16), vc) + _dot(pp.astype(BF16), vp)) / l)
            lses.append(m + jnp.log(l))
        o_ref[:, sl] = jnp.where(lo, outs[0], outs[1])
        lse_ref[:, sl] = jnp.where(lo, lses[0], lses[1])


def _dil_band(q, kv, dil):
    b, t, _ = q.shape
    ls = t // dil
    nbk = ls // DIL_BLOCK
    qv = q.reshape(b, ls, dil * HD)
    kvv = kv.reshape(b, ls, dil * 2 * HD)
    blk = lambda f: pl.BlockSpec((None, DIL_BLOCK, HD), f)
    cur = lambda off: (lambda bi, r, i: (bi, i, 2 * r + off))
    prev = lambda off: (lambda bi, r, i: (bi, jnp.maximum(i - 1, 0), 2 * r + off))
    o, lse = pl.pallas_call(
        _dil_body,
        out_shape=[jax.ShapeDtypeStruct((b, ls, dil * HD), F32)] * 2,
        grid=(b, dil, nbk),
        in_specs=[blk(lambda bi, r, i: (bi, i, r)), blk(prev(0)), blk(cur(0)), blk(prev(1)), blk(cur(1))],
        out_specs=[blk(lambda bi, r, i: (bi, i, r))] * 2,
        compiler_params=_cp("arbitrary", "arbitrary", "arbitrary"),
        name="dil_band",
    )(qv, kvv, kvv, kvv, kvv)
    return o.reshape(b, t, HD), lse.reshape(b, t, HD)


APG = 4


def _heads3(x):
    return x.reshape(x.shape[0] // HEAD_DIM, HEAD_DIM, x.shape[1])


def _paged_attn_body(pt_ref, q_ref, *refs):
    pages = refs[:APG]
    bias_ref, kvn_ref, bn_ref, o_ref, m_scr, l_scr, acc_scr = refs[APG:]
    s_idx = pl.program_id(1)
    q3 = _heads3(q_ref[...])
    lane0 = _iota((N_HEADS, LANES), 1) == 0

    @pl.when(s_idx == 0)
    def _():
        s_new = jnp.sum(q3 * _heads3(kvn_ref[0:HD, :]), axis=1) + bn_ref[...]
        m_scr[...] = s_new
        l_scr[...] = jnp.where(lane0, 1.0, 0.0)
        acc_scr[...] = jnp.where(_iota((HD, LANES), 1) == 0, kvn_ref[HD:2 * HD, :], 0.0)

    bh = bias_ref.shape[0]
    for h in range(N_HEADS):
        rk = slice(h * HEAD_DIM, (h + 1) * HEAD_DIM)
        rv = slice(HD + h * HEAD_DIM, HD + (h + 1) * HEAD_DIM)
        hb = h if bh > 1 else 0
        qh = q_ref[rk, :]
        m_old = m_scr[h:h + 1, :]
        ss = [jnp.sum(pages[p][rk, :] * qh, axis=0, keepdims=True)
              + bias_ref[hb:hb + 1, p * PAGE_SIZE:(p + 1) * PAGE_SIZE] for p in range(APG)]
        mn = jnp.maximum(m_old, jnp.max(functools.reduce(jnp.maximum, ss), axis=1, keepdims=True))
        alpha = jnp.exp(m_old - mn)
        acc = alpha * acc_scr[rk, :]
        l = alpha * l_scr[h:h + 1, :]
        for p in range(APG):
            pr = jnp.exp(ss[p] - mn)
            l = l + pr
            acc = acc + pr * pages[p][rv, :]
        m_scr[h:h + 1, :] = mn
        l_scr[h:h + 1, :] = l
        acc_scr[rk, :] = acc

    @pl.when(s_idx == pl.num_programs(1) - 1)
    def _():
        lsum = jnp.sum(l_scr[...], axis=1, keepdims=True)
        o3 = jnp.sum(_heads3(acc_scr[...]), axis=2, keepdims=True) / lsum[:, :, None]
        o_ref[...] = jnp.broadcast_to(o3, (N_HEADS, HEAD_DIM, LANES)).reshape(HD, LANES)


def _paged_attn(page_table, q_rep, cache_t, bias, kvn_rep, bias_new):
    bs, n_pages = page_table.shape
    bh = bias.shape[1]
    page_spec = lambda p: pl.BlockSpec(
        (None, 2 * HD, PAGE_SIZE), lambda bi, s, pt: (pt[bi * n_pages + s * APG + p], 0, 0))
    grid_spec = pltpu.PrefetchScalarGridSpec(
        num_scalar_prefetch=1,
        grid=(bs, n_pages // APG),
        in_specs=[pl.BlockSpec((None, HD, LANES), lambda bi, s, pt: (bi, 0, 0))]
        + [page_spec(p) for p in range(APG)]
        + [pl.BlockSpec((None, bh, APG * PAGE_SIZE), lambda bi, s, pt: (bi, 0, s)),
           pl.BlockSpec((None, 2 * HD, LANES), lambda bi, s, pt: (bi, 0, 0)),
           pl.BlockSpec((None, bh, LANES), lambda bi, s, pt: (bi, 0, 0))],
        out_specs=pl.BlockSpec((None, HD, LANES), lambda bi, s, pt: (bi, 0, 0)),
        scratch_shapes=[pltpu.VMEM((N_HEADS, LANES), F32), pltpu.VMEM((N_HEADS, LANES), F32),
                        pltpu.VMEM((HD, LANES), F32)],
    )
    return pl.pallas_call(
        _paged_attn_body,
        out_shape=jax.ShapeDtypeStruct((bs, HD, LANES), F32),
        grid_spec=grid_spec,
        compiler_params=_cp("arbitrary", "arbitrary"),
        name="paged_attn",
    )(page_table.reshape(-1), q_rep, *([cache_t] * APG), bias, kvn_rep, bias_new)


def _top3_lanes(g, blk):
    idx = []
    for _ in range(MOBA_TOPK):
        m = jnp.max(g, axis=1, keepdims=True)
        ix = jnp.min(jnp.where(g == m, blk, 1e9), axis=1, keepdims=True)
        idx.append(ix)
        g = jnp.where(blk == ix, -3e38, g)
    return idx


GBS = 4
GPB = MOBA_BLOCK // PAGE_SIZE


def _moba_gate_body(pt_ref, q_ref, *refs):
    pages = refs[:GBS * GPB]
    bias_ref, gate_scr = refs[GBS * GPB:]
    n = pl.program_id(1)
    nb = pl.num_programs(1)
    past = bias_ref.shape[1]
    lane = _iota((N_HEADS, LANES), 1)
    q3 = _heads3(q_ref[...])

    @pl.when(n == 0)
    def _():
        gate_scr[...] = jnp.full(gate_scr.shape, NEG_INF, F32)

    gates = gate_scr[...]
    for k in range(GBS):
        ksum = functools.reduce(lambda a, b: a + b, [pages[k * GPB + p][...] for p in range(GPB)])
        g = jnp.sum(jnp.sum(_heads3(ksum) * q3, axis=1), axis=1, keepdims=True) * (1.0 / MOBA_BLOCK)
        gates = jnp.where(lane == n * GBS + k, g, gates)
    gate_scr[...] = gates

    @pl.when(n == nb - 1)
    def _():
        i1, i2, i3 = _top3_lanes(gate_scr[...], lane.astype(F32))
        blk = (_iota((N_HEADS, past), 1) // MOBA_BLOCK).astype(F32)
        bias_ref[...] = jnp.where((blk == i1) | (blk == i2) | (blk == i3), 0.0, NEG_INF)


def _moba_gate(page_table, q_rep, cache_t):
    bs, n_pages = page_table.shape
    pps = GBS * GPB
    past = n_pages * PAGE_SIZE
    page_spec = lambda p: pl.BlockSpec(
        (None, HD, PAGE_SIZE), lambda bi, n, pt: (pt[bi * n_pages + n * pps + p], 0, 0))
    grid_spec = pltpu.PrefetchScalarGridSpec(
        num_scalar_prefetch=1,
        grid=(bs, n_pages // pps),
        in_specs=[pl.BlockSpec((None, HD, LANES), lambda bi, n, pt: (bi, 0, 0))] + [page_spec(p) for p in range(pps)],
        out_specs=pl.BlockSpec((None, N_HEADS, past), lambda bi, n, pt: (bi, 0, 0)),
        scratch_shapes=[pltpu.VMEM((N_HEADS, LANES), F32)],
    )
    return pl.pallas_call(
        _moba_gate_body,
        out_shape=jax.ShapeDtypeStruct((bs, N_HEADS, past), F32),
        grid_spec=grid_spec,
        compiler_params=_cp("arbitrary", "arbitrary"),
        name="moba_gate",
    )(page_table.reshape(-1), q_rep, *([cache_t] * pps))


SPG = 8


def _dsa_score_body(pt_ref, qi_ref, wi_ref, *refs):
    pages = refs[:SPG]
    out_ref = refs[SPG]
    qi = qi_ref[...]
    wi = wi_ref[...]
    for p in range(SPG):
        d = _dot(qi, pages[p][...].astype(BF16))
        out_ref[:, p * PAGE_SIZE:(p + 1) * PAGE_SIZE] = jnp.sum(wi * jnp.maximum(d, 0.0), axis=0, keepdims=True)


def _dsa_scores(page_table, qi, wi, kidx_t):
    bs, n_pages = page_table.shape
    page_spec = lambda p: pl.BlockSpec(
        (None, DSA_IDX_DIM, PAGE_SIZE), lambda bi, s, pt: (pt[bi * n_pages + s * SPG + p], 0, 0))
    grid_spec = pltpu.PrefetchScalarGridSpec(
        num_scalar_prefetch=1,
        grid=(bs, n_pages // SPG),
        in_specs=[pl.BlockSpec((None, DSA_IDX_HEADS, DSA_IDX_DIM), lambda bi, s, pt: (bi, 0, 0)),
                  pl.BlockSpec((None, DSA_IDX_HEADS, 1), lambda bi, s, pt: (bi, 0, 0))]
        + [page_spec(p) for p in range(SPG)],
        out_specs=pl.BlockSpec((None, 1, SPG * PAGE_SIZE), lambda bi, s, pt: (bi, 0, s)),
    )
    return pl.pallas_call(
        _dsa_score_body,
        out_shape=jax.ShapeDtypeStruct((bs, 1, n_pages * PAGE_SIZE), F32),
        grid_spec=grid_spec,
        compiler_params=_cp("arbitrary", "arbitrary"),
        name="dsa_scores",
    )(page_table.reshape(-1), qi, wi, *([kidx_t] * SPG))


def _dsa_select_body(sc_ref, qi_ref, kiwi_ref, bias_ref, bn_ref, key_scr, lim_scr, *, n_keep):
    bs, past = sc_ref.shape
    nch = past // ICH
    for c in range(nch):
        key_scr[:, c * ICH:(c + 1) * ICH] = _sort_key(sc_ref[:, c * ICH:(c + 1) * ICH])
    kiwi = kiwi_ref[...]
    qi = qi_ref[...].astype(F32)
    s_new = jnp.zeros((bs, 1), F32)
    for h in range(DSA_IDX_HEADS):
        d = jnp.sum(qi[:, h * DSA_IDX_DIM:(h + 1) * DSA_IDX_DIM] * kiwi[:, 0:DSA_IDX_DIM],
                    axis=1, keepdims=True)
        s_new = s_new + kiwi[:, DSA_IDX_DIM + h:DSA_IDX_DIM + h + 1] * jnp.maximum(d, 0.0)
    key_new = _sort_key(s_new)

    kf = float(n_keep)
    count = _counter(key_scr, nch, 1)
    thr = _kth_largest_key(count, (bs, 1), kf, extra_key=key_new)
    n_gt = count(lambda kc, idx: kc > thr) + jnp.where(key_new > thr, 1.0, 0.0)
    n_eq = count(lambda kc, idx: kc == thr)
    need = kf - n_gt
    lim_scr[...] = jnp.full(lim_scr.shape, past, I32)
    has_tie = jnp.max(jnp.where(n_eq > need, 1.0, 0.0)) > 0.5

    @pl.when(has_tie)
    def _():
        lim_scr[...] = jnp.broadcast_to(_tie_limit(count, (bs, 1), thr, need, past.bit_length()), lim_scr.shape)

    lim = lim_scr[:, 0:1]
    for c in range(nch):
        kc = key_scr[:, c * ICH:(c + 1) * ICH]
        kpos = c * ICH + _iota((bs, ICH), 1)
        tie_ok = jnp.where(kc == thr, jnp.where(kpos <= lim, 0.0, NEG_INF), NEG_INF)
        bias_ref[:, c * ICH:(c + 1) * ICH] = jnp.where(kc > thr, 0.0, tie_ok)
    new_ok = (key_new > thr) | ((key_new == thr) & (n_eq < need))
    bn_ref[...] = jnp.broadcast_to(jnp.where(new_ok, 0.0, NEG_INF), bn_ref.shape)


def _dsa_select(scores, qi, kiwi, n_keep):
    bs, past = scores.shape
    return pl.pallas_call(
        functools.partial(_dsa_select_body, n_keep=n_keep),
        out_shape=[jax.ShapeDtypeStruct((bs, past), F32), jax.ShapeDtypeStruct((bs, LANES), F32)],
        scratch_shapes=[pltpu.VMEM((bs, past), I32), pltpu.VMEM((bs, LANES), I32)],
        compiler_params=pltpu.CompilerParams(vmem_limit_bytes=VMEM_LIMIT),
        name="dsa_select",
    )(scores, qi, kiwi)


FPG = 8


def _fox_bias_body(pt_ref, lfn_ref, *refs):
    pages = refs[:FPG]
    bias_ref, carry_scr = refs[FPG:]

    @pl.when(pl.program_id(1) == 0)
    def _():
        carry_scr[...] = lfn_ref[...]

    later = (_iota((PAGE_SIZE, PAGE_SIZE), 0) > _iota((PAGE_SIZE, PAGE_SIZE), 1)).astype(F32)
    carry = carry_scr[...]
    for p in range(FPG - 1, -1, -1):
        x = pages[p][...]
        bias_ref[:, p * PAGE_SIZE:(p + 1) * PAGE_SIZE] = _dot(x, later, precision=HIGHEST) + carry
        carry = carry + jnp.sum(x, axis=1, keepdims=True)
    carry_scr[...] = carry


def _fox_bias(page_table, logf_t, lf_new):
    bs, n_pages = page_table.shape
    nst = n_pages // FPG
    page_spec = lambda p: pl.BlockSpec(
        (None, N_HEADS, PAGE_SIZE),
        lambda bi, s, pt: (pt[bi * n_pages + (nst - 1 - s) * FPG + p], 0, 0))
    grid_spec = pltpu.PrefetchScalarGridSpec(
        num_scalar_prefetch=1,
        grid=(bs, nst),
        in_specs=[pl.BlockSpec((None, N_HEADS, LANES), lambda bi, s, pt: (bi, 0, 0))]
        + [page_spec(p) for p in range(FPG)],
        out_specs=pl.BlockSpec((None, N_HEADS, FPG * PAGE_SIZE), lambda bi, s, pt: (bi, 0, nst - 1 - s)),
        scratch_shapes=[pltpu.VMEM((N_HEADS, LANES), F32)],
    )
    return pl.pallas_call(
        _fox_bias_body,
        out_shape=jax.ShapeDtypeStruct((bs, N_HEADS, n_pages * PAGE_SIZE), F32),
        grid_spec=grid_spec,
        compiler_params=_cp("arbitrary", "arbitrary"),
        name="fox_bias",
    )(page_table.reshape(-1), lf_new, *([logf_t] * FPG))


def _dil_sample_body(*refs):
    n_grp = len(DIL_WINDOWS)
    q_refs = refs[:n_grp]
    kvn_refs = refs[n_grp:2 * n_grp]
    buf_refs = refs[2 * n_grp:3 * n_grp]
    y_ref = refs[3 * n_grp]
    win_refs = refs[3 * n_grp + 1:]
    outs, lses = [], []
    for g in range(n_grp):
        dil = DIL_DILATIONS[g]
        w = buf_refs[g].shape[2]
        q3 = _heads3(q_refs[g][...])
        kn = kvn_refs[g][0]
        vn = kvn_refs[g][1]
        kb = buf_refs[g][0]
        vb = buf_refs[g][1]
        nt = w // LANES
        s_new = jnp.sum(q3 * _heads3(kn), axis=1)[:, 0:1]
        ss = [jnp.sum(_heads3(kb[:, c * LANES:(c + 1) * LANES]) * q3, axis=1) for c in range(nt)]
        lane = _iota((2, LANES), 1)
        ss = [jnp.where(((c * LANES + lane) % dil) == 0, s, NEG_INF) for c, s in enumerate(ss)]
        m = functools.reduce(jnp.maximum, [jnp.max(s, axis=1, keepdims=True) for s in ss] + [s_new])
        ps = [jnp.exp(s - m) for s in ss]
        p_new = jnp.exp(s_new - m)
        l = functools.reduce(lambda a, b: a + b, [jnp.sum(p, axis=1, keepdims=True) for p in ps]) + p_new
        acc = functools.reduce(lambda a, b: a + b,
                               [p[:, None, :] * _heads3(vb[:, c * LANES:(c + 1) * LANES]) for c, p in enumerate(ps)])
        o = jnp.sum(acc, axis=2, keepdims=True) + p_new[:, :, None] * _heads3(vn)[:, :, 0:1]
        outs.append(o / l[:, :, None])
        lses.append(m + jnp.log(l))
        last = _iota((LANES, w), 1) == w - 1
        win_refs[g][0] = jnp.where(last, kn[:, 0:1], pltpu.roll(kb, w - 1, 1))
        win_refs[g][1] = jnp.where(last, vn[:, 0:1], pltpu.roll(vb, w - 1, 1))
    m = functools.reduce(jnp.maximum, lses)
    es = [jnp.exp(l - m) for l in lses]
    den = functools.reduce(lambda a, b: a + b, es)
    y = functools.reduce(lambda a, b: a + b, [(e / den)[:, :, None] * o for e, o in zip(es, outs)])
    y_ref[...] = jnp.broadcast_to(y, (2, HEAD_DIM, LANES)).reshape(LANES, LANES)


def _dil_sample(q_reps, kvn_reps, bufs_t):
    bs = q_reps[0].shape[0]
    nhp = HD // LANES
    args, specs = [], []
    for q in q_reps:
        args.append(q)
        specs.append(pl.BlockSpec((None, LANES, LANES), lambda bi, hp: (bi, hp, 0)))
    for kvn in kvn_reps:
        args.append(kvn)
        specs.append(pl.BlockSpec((None, 2, LANES, LANES), lambda bi, hp: (bi, 0, hp, 0)))
    win_specs, win_shapes = [], []
    for buf in bufs_t:
        w = buf.shape[3]
        args.append(buf)
        spec = pl.BlockSpec((None, 2, LANES, w), lambda bi, hp: (bi, 0, hp, 0))
        specs.append(spec)
        win_specs.append(spec)
        win_shapes.append(jax.ShapeDtypeStruct(buf.shape, F32))
    return pl.pallas_call(
        _dil_sample_body,
        out_shape=[jax.ShapeDtypeStruct((bs, HD, LANES), F32)] + win_shapes,
        grid=(bs, nhp),
        in_specs=specs,
        out_specs=[pl.BlockSpec((None, LANES, LANES), lambda bi, hp: (bi, hp, 0))] + win_specs,
        compiler_params=_cp("arbitrary", "arbitrary"),
        name="dil_sample",
    )(*args)


def _rope_angles(pos):
    inv = ROPE_THETA ** (-jnp.arange(HALF, dtype=F32) / HALF)
    ang = pos.astype(F32)[:, None] * inv
    return jnp.cos(ang), jnp.sin(ang)


def _rope_tables(pos):
    cos, sin = _rope_angles(pos)
    return (jnp.concatenate([cos, cos, cos, cos], -1), jnp.concatenate([-sin, sin, -sin, sin], -1))


def _pad_cols(w, n):
    return jnp.pad(w, ((0, 0), (0, n - w.shape[1])))


def _kv5(kv):
    return kv.reshape(kv.shape[0], kv.shape[1], 2, N_HEADS, HEAD_DIM)


def _kv5_t(kvt):
    b, _, t = kvt.shape
    return kvt.reshape(b, 2, N_HEADS, HEAD_DIM, t).transpose(0, 4, 1, 2, 3)


def _chan_major(x5):
    n, rows = x5.shape[:2]
    return x5.transpose(0, 2, 3, 4, 1).reshape(n, 2 * HD, rows)


def _lane_rep(x):
    return jnp.broadcast_to(x.astype(F32)[..., None], x.shape + (LANES,))


def kernel(x_prompt, x_sample, c_prompt, c_sample, cache_kv_moba, cache_kv_dsa, cache_kidx_dsa, cache_kv_fox, cache_logf_fox, state_win_d1, state_win_d4, state_win_d16, page_table, w_ada, b_ada, ln_g, ln_b, ffn_w_up, ffn_w_down, moba_w_qkv, moba_w_o, dsa_w_in, dsa_w_o, fox_w_in, fox_b_f, fox_w_o, dil_w_qkv, dil_w_o):
    bp, seq, d = x_prompt.shape
    bs = x_sample.shape[0]
    past = page_table.shape[1] * PAGE_SIZE
    depth = w_ada.shape[0]

    mp = -(-(bp + bs) // SUBLANES) * SUBLANES
    c_all = jnp.pad(jnp.concatenate([c_prompt, c_sample], 0), ((0, mp - bp - bs), (0, 0)))
    ada_all = _ada_all(c_all, w_ada, b_ada)
    ada_p = ada_all[:, :bp].reshape(depth, bp, N_ADA, 1, d)
    ada_s = ada_all[:, bp:bp + bs]

    pos_p = jnp.arange(seq, dtype=I32)
    cos_p, sin_p = _rope_tables(pos_p)
    cos_a, sin_a = _rope_angles(pos_p)
    tabs_p = (cos_p, sin_p, cos_a.T, sin_a.T)
    tabs_id = (jnp.ones((seq, LANES), F32), jnp.zeros((seq, LANES), F32),
               jnp.ones((HALF, seq), F32), jnp.zeros((HALF, seq), F32))
    cos_s, sin_s = _rope_tables(jnp.full((1,), past, I32))
    one_c, zero_s = jnp.ones((1, LANES), F32), jnp.zeros((1, LANES), F32)

    w_up = ffn_w_up.astype(BF16)
    w_down = ffn_w_down.astype(BF16)
    w_moba = moba_w_qkv.astype(BF16)
    w_dsa = _pad_cols(dsa_w_in, 3 * HD + DSA_NQI + LANES).astype(BF16)
    w_fox = _pad_cols(fox_w_in, 3 * HD + LANES).astype(BF16)
    w_dil = dil_w_qkv.astype(BF16)
    b_fox = jnp.pad(fox_b_f, (0, LANES - N_HEADS)).reshape(1, LANES)
    w_outs = [w.astype(BF16) for w in (moba_w_o, dsa_w_o, fox_w_o, dil_w_o)]
    zero_bn = jnp.zeros((bs, N_HEADS, LANES), F32)

    xp = x_prompt
    xs = x_sample.reshape(1, bs, d)
    res = {}
    for i in range(depth):
        ap, as_ = ada_p[i], ada_s[i]
        xp = _ffn(xp, ap, 0, w_up[i, 0], w_down[i, 0], ln_g[i, 0], ln_b[i, 0], False)
        xs = _ffn(xs, as_, 0, w_up[i, 0], w_down[i, 0], ln_g[i, 0], ln_b[i, 0], True)
        kind = i % 4
        lp = ()
        if kind == 0:
            qt, kn, kvt, kmean8 = _projt(xp, ap, moba_w_qkv, tabs_p, True, "moba")
            kmean = kmean8[:, :, :PROJT_TM // MOBA_BLOCK].reshape(bp, seq // MOBA_BLOCK, HD)
            op = [_flasht(qt, kn, kvt, "moba", (kmean,))]
            qs, kvs = _proj(xs, as_, w_moba, cos_s, sin_s, True, True)
            cache_t = _chan_major(cache_kv_moba)
            q_rep = _lane_rep(qs[0])
            bias = _moba_gate(page_table, q_rep, cache_t)
            o_rep = _paged_attn(page_table, q_rep, cache_t, bias, _lane_rep(kvs[0]), zero_bn)
            res["kv_moba"] = (_kv5_t(kvt), kvs)
        elif kind == 1:
            qt, kn, kvt, qit, kin, kit, wit = _projt(xp, ap, dsa_w_in, tabs_p, True, "dsa")
            mask = _dsa_index(qit, wit, kin, min(DSA_TOPK_MAX, seq // 4))
            op = [_flasht(qt, kn, kvt, "dsa", (mask,))]
            qs, kvs, qis, kiwis = _proj(xs, as_, w_dsa, cos_s, sin_s, True, True, "dsa")
            wi = kiwis[0, :, DSA_IDX_DIM:DSA_IDX_DIM + DSA_IDX_HEADS].reshape(bs, DSA_IDX_HEADS, 1)
            scores = _dsa_scores(page_table, qis[0].reshape(bs, DSA_IDX_HEADS, DSA_IDX_DIM), wi,
                                 cache_kidx_dsa.transpose(0, 2, 1))
            bias, bias_new = _dsa_select(scores.reshape(bs, past), qis[0], kiwis[0],
                                         min(DSA_TOPK_MAX, (past + 1) // 4))
            o_rep = _paged_attn(page_table, _lane_rep(qs[0]), _chan_major(cache_kv_dsa), bias.reshape(bs, 1, past),
                                _lane_rep(kvs[0]), bias_new.reshape(bs, 1, LANES))
            res["kv_dsa"] = (_kv5_t(kvt), kvs)
            res["kidx_dsa"] = (kit.transpose(0, 2, 1), kiwis[..., :DSA_IDX_DIM].reshape(bs, 1, DSA_IDX_DIM))
        elif kind == 2:
            qt, kn, kvt, lft, ft, fn = _projt(xp, ap, fox_w_in, tabs_id, False, "fox", fox_b_f)
            op = [_flasht(qt, kn, kvt, "fox", (fn, ft))]
            qs, kvs, lfs = _proj(xs, as_, w_fox, one_c, zero_s, True, False, "fox", b_fox)
            lf_new = jnp.broadcast_to(lfs[0, :, :N_HEADS, None], (bs, N_HEADS, LANES))
            bias = _fox_bias(page_table, cache_logf_fox.transpose(0, 2, 1), lf_new)
            o_rep = _paged_attn(page_table, _lane_rep(qs[0]), _chan_major(cache_kv_fox), bias,
                                _lane_rep(kvs[0]), zero_bn)
            res["kv_fox"] = (_kv5_t(kvt), kvs)
            res["logf_fox"] = (lft.transpose(0, 2, 1), lfs[..., :N_HEADS].reshape(bs, 1, N_HEADS))
        else:
            op, lp, q_reps, kvn_reps, bufs_t = [], [], [], [], []
            bufs = (state_win_d1, state_win_d4, state_win_d16)
            for g, (win, dil) in enumerate(zip(DIL_WINDOWS, DIL_DILATIONS)):
                wg = w_dil[:, g * 3 * HD:(g + 1) * 3 * HD]
                qp, kvp = _proj(xp, ap, wg, cos_p, sin_p, False, True)
                o, lse = _dil_band(qp, kvp, dil)
                op.append(o)
                lp.append(lse)
                qs, kvs = _proj(xs, as_, wg, cos_s, sin_s, True, True)
                q_reps.append(_lane_rep(qs[0]))
                kvn_reps.append(_lane_rep(kvs[0]).reshape(bs, 2, HD, LANES))
                bufs_t.append(_chan_major(bufs[g]).reshape(bs, 2, HD, bufs[g].shape[1]))
                res["win_%d" % g] = [_kv5(kvp[:, seq - min(win, seq):])]
            o_rep, *wins = _dil_sample(q_reps, kvn_reps, bufs_t)
            for g, wn in enumerate(wins):
                res["win_%d" % g].append(_kv5_t(wn.reshape(bs, 2 * HD, wn.shape[3])))
        os_ = [o_rep[:, :, 0].reshape(1, bs, HD)]
        xp = _oproj(xp, ap, op, lp, w_outs[kind], ln_g[i, 1], ln_b[i, 1], False)
        xs = _oproj(xs, as_, os_, (), w_outs[kind], ln_g[i, 1], ln_b[i, 1], True)
        xp = _ffn(xp, ap, 2, w_up[i, 1], w_down[i, 1], ln_g[i, 2], ln_b[i, 2], False)
        xs = _ffn(xs, as_, 2, w_up[i, 1], w_down[i, 1], ln_g[i, 2], ln_b[i, 2], True)

    kv_out = lambda name: (res[name][0], _kv5(res[name][1].reshape(bs, 1, 2 * HD)))
    return (xp, xs.reshape(bs, 1, d),
            *kv_out("kv_moba"), *kv_out("kv_dsa"), *res["kidx_dsa"],
            *kv_out("kv_fox"), *res["logf_fox"],
            *res["win_0"], *res["win_1"], *res["win_2"])
```

```python
import functools

import jax
import jax.numpy as jnp
from jax import lax
from jax.experimental import pallas as pl
from jax.experimental.pallas import tpu as pltpu

F32 = jnp.float32
BF16 = jnp.bfloat16
I32 = jnp.int32
HIGHEST = lax.Precision.HIGHEST

D_MODEL = 1024
N_HEADS = 16
HEAD_DIM = 64
HALF = HEAD_DIM // 2
HD = N_HEADS * HEAD_DIM
D_FF = 2816
N_ADA = 9
DEPTH = 4
PAGE_SIZE = 128
FFN_RES_WEIGHT = 0.5
ROPE_THETA = 10000.0
LN_EPS = 1e-5
NEG_INF = -1e30
DN_ALPHA = (2 * DEPTH) ** 0.25
MOBA_BLOCK = 256
MOBA_TOPK = 3
DSA_TOPK_MAX = 256
DSA_IDX_HEADS = 8
DSA_IDX_DIM = 64
DSA_NQI = DSA_IDX_HEADS * DSA_IDX_DIM
DSA_W_SCALE = (DSA_IDX_HEADS ** -0.5) * (DSA_IDX_DIM ** -0.5)
DIL_WINDOWS = (128, 512, 2048)
DIL_DILATIONS = (1, 4, 16)
DIL_BLOCK = 128
QK_SCALE = HEAD_DIM ** -0.5

LANES = 128
SUBLANES = 8
VMEM_LIMIT = 56 * 2 ** 20
INT_MIN = -2 ** 31
NEG_KEY = -1900671691

_NT = (((1,), (1,)), ((), ()))


def _cp(*sem):
    return pltpu.CompilerParams(dimension_semantics=sem, vmem_limit_bytes=VMEM_LIMIT)


def _dot(a, b, **kw):
    return jnp.dot(a, b, preferred_element_type=F32, **kw)


def _dot_nt(a, b, **kw):
    return lax.dot_general(a, b, _NT, preferred_element_type=F32, **kw)


def _iota(shape, dim):
    return lax.broadcasted_iota(I32, shape, dim)


def _sort_key(x):
    x = jnp.where(x == 0.0, 0.0, x)
    bits = pltpu.bitcast(x, I32)
    return jnp.where(bits < 0, bits ^ 0x7FFFFFFF, bits)


def _log_sigmoid(z):
    return jnp.minimum(z, 0.0) - jnp.log1p(jnp.exp(-jnp.abs(z)))


def _post_ln(x, y, gate, g, b, weight):
    r = DN_ALPHA * x + (weight * (1.0 + gate)) * y
    mu = jnp.mean(r, axis=-1, keepdims=True)
    rc = r - mu
    var = jnp.mean(rc * rc, axis=-1, keepdims=True)
    return rc * lax.rsqrt(var + LN_EPS) * g + b


def _ada_operand(ada, j, k, per_row):
    if per_row:
        m = ada.shape[0]
        return ada, pl.BlockSpec((m, D_MODEL), lambda *g: (0, 3 * j + k))
    return ada, pl.BlockSpec((None, None, 1, D_MODEL), lambda *g: (g[0], 3 * j + k, 0, 0))


def _ada_body(c_ref, w_ref, b_ref, o_ref):
    o_ref[...] = _dot(c_ref[...].astype(BF16), w_ref[...].astype(BF16)) + b_ref[...]


def _ada_all(c_all, w_ada, b_ada):
    m = c_all.shape[0]
    depth, d, n = w_ada.shape
    tn = 1152
    return pl.pallas_call(
        _ada_body,
        out_shape=jax.ShapeDtypeStruct((depth, m, n), F32),
        grid=(depth, n // tn),
        in_specs=[pl.BlockSpec((m, d), lambda l, c: (0, 0)),
                  pl.BlockSpec((None, d, tn), lambda l, c: (l, 0, c)),
                  pl.BlockSpec((None, 1, tn), lambda l, c: (l, 0, c))],
        out_specs=pl.BlockSpec((None, m, tn), lambda l, c: (l, 0, c)),
        compiler_params=_cp("arbitrary", "arbitrary"),
        name="ada",
    )(c_all, w_ada, b_ada.reshape(depth, 1, n))


def _ffn_body(x_ref, sh_ref, sc_ref, gt_ref, wg_ref, wu_ref, wd_ref, lg_ref, lb_ref, o_ref,
              h_scr, acc_scr):
    f = pl.program_id(2)

    @pl.when(f == 0)
    def _():
        h_scr[...] = (x_ref[...] * (1.0 + sc_ref[...]) + sh_ref[...]).astype(BF16)

    h = h_scr[...]
    g = _dot(h, wg_ref[...])
    u = _dot(h, wu_ref[...])
    a = (g * jax.nn.sigmoid(g) * u).astype(BF16)
    y = _dot(a, wd_ref[...])

    @pl.when(f == 0)
    def _():
        acc_scr[...] = y

    @pl.when(f > 0)
    def _():
        acc_scr[...] += y

    @pl.when(f == pl.num_programs(2) - 1)
    def _():
        o_ref[...] = _post_ln(x_ref[...], acc_scr[...], gt_ref[...], lg_ref[...], lb_ref[...],
                              FFN_RES_WEIGHT)


def _ffn(x, ada, j, w_up, w_down, ln_g, ln_b, per_row):
    b, t, d = x.shape
    ff = w_down.shape[0]
    tm = min(t, 512)
    fc = 1408 if ff % 1408 == 0 else ff
    nf = ff // fc
    sh, sh_spec = _ada_operand(ada, j, 0, per_row)
    sc, sc_spec = _ada_operand(ada, j, 1, per_row)
    gt, gt_spec = _ada_operand(ada, j, 2, per_row)
    return pl.pallas_call(
        _ffn_body,
        out_shape=jax.ShapeDtypeStruct((b, t, d), F32),
        grid=(b, t // tm, nf),
        in_specs=[pl.BlockSpec((None, tm, d), lambda bi, i, f: (bi, i, 0)),
                  sh_spec, sc_spec, gt_spec,
                  pl.BlockSpec((d, fc), lambda bi, i, f: (0, f)),
                  pl.BlockSpec((d, fc), lambda bi, i, f: (0, f + nf)),
                  pl.BlockSpec((fc, d), lambda bi, i, f: (f, 0)),
                  pl.BlockSpec((1, d), lambda bi, i, f: (0, 0)),
                  pl.BlockSpec((1, d), lambda bi, i, f: (0, 0))],
        out_specs=pl.BlockSpec((None, tm, d), lambda bi, i, f: (bi, i, 0)),
        scratch_shapes=[pltpu.VMEM((tm, d), BF16), pltpu.VMEM((tm, d), F32)],
        compiler_params=_cp("arbitrary", "arbitrary", "arbitrary"),
        name="ffn",
    )(x, sh, sc, gt, w_up, w_up, w_down, ln_g.reshape(1, d), ln_b.reshape(1, d))


def _rope_lanes(zc, cos, sin, first_half):
    sw = jnp.where(first_half, pltpu.roll(zc, 96, 1), pltpu.roll(zc, 32, 1))
    return zc * cos + sw * sin


def _rope_rows_store(ref, row0, z, cos, sin, scale=None, dtype=F32):
    for h in range(z.shape[0] // HEAD_DIM):
        x1 = z[h * HEAD_DIM:h * HEAD_DIM + HALF, :]
        x2 = z[h * HEAD_DIM + HALF:(h + 1) * HEAD_DIM, :]
        o1 = x1 * cos - x2 * sin
        o2 = x2 * cos + x1 * sin
        if scale is not None:
            o1, o2 = o1 * scale, o2 * scale
        r = row0 + h * HEAD_DIM
        ref[r:r + HALF, :] = o1.astype(dtype)
        ref[r + HALF:r + HEAD_DIM, :] = o2.astype(dtype)


def _proj_body(*refs, rope, extra):
    x_ref, sh_ref, sc_ref, cos_ref, sin_ref, w_ref = refs[:6]
    rest = refs[6:]
    if extra == "fox":
        bf_ref, q_ref, kv_ref, lf_ref = rest
    elif extra == "dsa":
        q_ref, kv_ref, qi_ref, kiwi_ref = rest
    else:
        q_ref, kv_ref = rest
    tm = x_ref.shape[0]
    h = (x_ref[...] * (1.0 + sc_ref[...]) + sh_ref[...]).astype(BF16)
    cos = cos_ref[...]
    sin = sin_ref[...]
    lane = _iota((tm, LANES), 1)
    first_half = (lane & HALF) == 0

    zq = _dot(h, w_ref[:, 0:HD])
    for s in range(HD // LANES):
        sl = slice(s * LANES, (s + 1) * LANES)
        c = zq[:, sl]
        q_ref[:, sl] = ((_rope_lanes(c, cos, sin, first_half) if rope else c) * QK_SCALE).astype(q_ref.dtype)
    zk = _dot(h, w_ref[:, HD:2 * HD])
    for s in range(HD // LANES):
        sl = slice(s * LANES, (s + 1) * LANES)
        c = zk[:, sl]
        kv_ref[:, sl] = (_rope_lanes(c, cos, sin, first_half) if rope else c).astype(kv_ref.dtype)
    kv_ref[:, HD:2 * HD] = _dot(h, w_ref[:, 2 * HD:3 * HD]).astype(kv_ref.dtype)

    if extra == "dsa":
        ze = _dot(h, w_ref[:, 3 * HD:3 * HD + DSA_NQI + LANES])
        for s in range(DSA_NQI // LANES):
            sl = slice(s * LANES, (s + 1) * LANES)
            qi_ref[:, sl] = _rope_lanes(ze[:, sl], cos, sin, first_half).astype(BF16)
        c = ze[:, DSA_NQI:DSA_NQI + LANES]
        kiwi_ref[...] = jnp.where(lane < DSA_IDX_DIM, _rope_lanes(c, cos, sin, first_half), c * DSA_W_SCALE)
    elif extra == "fox":
        lf_ref[...] = _log_sigmoid(_dot(h, w_ref[:, 3 * HD:3 * HD + LANES]) + bf_ref[...])


def _proj(x, ada, w, cos, sin, per_row, rope, extra=None, b_f=None, qkv_dtype=(BF16, F32)):
    b, t, d = x.shape
    n = w.shape[1]
    tm = min(t, 512)
    sh, sh_spec = _ada_operand(ada, 1, 0, per_row)
    sc, sc_spec = _ada_operand(ada, 1, 1, per_row)
    tcs = cos.shape[0]
    cs_spec = (pl.BlockSpec((1, LANES), lambda bi, i: (0, 0)) if tcs == 1
               else pl.BlockSpec((tm, LANES), lambda bi, i: (i, 0)))
    row = lambda w_: pl.BlockSpec((None, tm, w_), lambda bi, i: (bi, i, 0))
    in_specs = [row(d), sh_spec, sc_spec, cs_spec, cs_spec,
                pl.BlockSpec((d, n), lambda bi, i: (0, 0))]
    args = [x, sh, sc, cos, sin, w]
    out_shape = [jax.ShapeDtypeStruct((b, t, HD), qkv_dtype[0]),
                 jax.ShapeDtypeStruct((b, t, 2 * HD), qkv_dtype[1])]
    out_specs = [row(HD), row(2 * HD)]
    if extra == "dsa":
        out_shape += [jax.ShapeDtypeStruct((b, t, DSA_NQI), BF16), jax.ShapeDtypeStruct((b, t, LANES), F32)]
        out_specs += [row(DSA_NQI), row(LANES)]
    elif extra == "fox":
        in_specs.append(pl.BlockSpec((1, LANES), lambda bi, i: (0, 0)))
        args.append(b_f)
        out_shape += [jax.ShapeDtypeStruct((b, t, LANES), F32)]
        out_specs += [row(LANES)]
    return pl.pallas_call(
        functools.partial(_proj_body, rope=rope, extra=extra),
        out_shape=out_shape,
        grid=(b, t // tm),
        in_specs=in_specs,
        out_specs=out_specs,
        compiler_params=_cp("arbitrary", "arbitrary"),
        name="proj_" + (extra or "qkv"),
    )(*args)


PROJT_TM = 512


def _projt_body(*refs, rope, extra):
    x_ref, sh_ref, sc_ref, cosn_ref, sinn_ref, cost_ref, sint_ref, wkn_ref, wt_ref = refs[:9]
    rest = refs[9:]
    if extra == "moba":
        qt_ref, kn_ref, kvt_ref, kmean_ref = rest
    elif extra == "dsa":
        wet_ref, wkin_ref, qt_ref, kn_ref, kvt_ref, qit_ref, kin_ref, kit_ref, wit_ref = rest
    else:
        wft_ref, wfn_ref, bft_ref, bfn_ref, qt_ref, kn_ref, kvt_ref, lft_ref, ft_ref, fn_ref, ct_scr, cn_scr = rest
    i = pl.program_id(1)
    tm = x_ref.shape[0]
    h = (x_ref[...] * (1.0 + sc_ref[...]) + sh_ref[...]).astype(BF16)
    lane = _iota((tm, LANES), 1)
    first_half = (lane & HALF) == 0
    if rope:
        cosn, sinn, cost, sint = cosn_ref[...], sinn_ref[...], cost_ref[...], sint_ref[...]

    zk = _dot(h, wkn_ref[...])
    kparts = []
    for s in range(HD // LANES):
        sl = slice(s * LANES, (s + 1) * LANES)
        c = zk[:, sl]
        c = _rope_lanes(c, cosn, sinn, first_half) if rope else c
        kn_ref[:, sl] = c.astype(BF16)
        kparts.append(c)
    if extra == "moba":
        kmean_ref[...] = jnp.zeros_like(kmean_ref)
        for nb in range(tm // MOBA_BLOCK):
            for s in range(HD // LANES):
                blk = kparts[s][nb * MOBA_BLOCK:(nb + 1) * MOBA_BLOCK, :]
                kmean_ref[nb:nb + 1, s * LANES:(s + 1) * LANES] = (
                    jnp.sum(blk, axis=0, keepdims=True) * (1.0 / MOBA_BLOCK))

    zq = _dot_nt(wt_ref[0:HD, :], h)
    if rope:
        _rope_rows_store(qt_ref, 0, zq, cost, sint, scale=QK_SCALE, dtype=BF16)
    else:
        qt_ref[...] = (zq * QK_SCALE).astype(BF16)
    zkt = _dot_nt(wt_ref[HD:2 * HD, :], h)
    if rope:
        _rope_rows_store(kvt_ref, 0, zkt, cost, sint)
    else:
        kvt_ref[0:HD, :] = zkt
    kvt_ref[HD:2 * HD, :] = _dot_nt(wt_ref[2 * HD:3 * HD, :], h)

    if extra == "dsa":
        ze = _dot_nt(wet_ref[...], h)
        _rope_rows_store(qit_ref, 0, ze[0:DSA_NQI, :], cost, sint, dtype=BF16)
        _rope_rows_store(kit_ref, 0, ze[DSA_NQI:DSA_NQI + DSA_IDX_DIM, :], cost, sint)
        wit_ref[...] = ze[DSA_NQI + DSA_IDX_DIM:, :] * DSA_W_SCALE
        zkin = _dot(h, wkin_ref[...])
        kin_ref[...] = jnp.where(lane < DSA_IDX_DIM, _rope_lanes(zkin, cosn, sinn, first_half), 0.0).astype(BF16)
    elif extra == "fox":
        @pl.when(i == 0)
        def _():
            ct_scr[...] = jnp.zeros_like(ct_scr)
            cn_scr[...] = jnp.zeros_like(cn_scr)

        lft = _log_sigmoid(_dot_nt(wft_ref[...], h) + bft_ref[...])
        lft_ref[...] = lft
        upper = (_iota((tm, tm), 0) <= _iota((tm, tm), 1)).astype(F32)
        ft = _dot(lft, upper, precision=HIGHEST) + ct_scr[:, 0:1]
        ft_ref[...] = ft
        ct_scr[...] = jnp.broadcast_to(ft[:, tm - 1:tm], ct_scr.shape)
        lfn = _log_sigmoid(_dot(h, wfn_ref[...]) + bfn_ref[...])
        lower = (_iota((tm, tm), 0) >= _iota((tm, tm), 1)).astype(F32)
        fn = _dot(lower, lfn, precision=HIGHEST) + cn_scr[...]
        fn_ref[...] = fn
        cn_scr[...] = fn[tm - 1:tm, :]


def _projt(x, ada, w, rope_tabs, rope, extra, b_f=None):
    b, t, d = x.shape
    tm = PROJT_TM
    cosn, sinn, cost, sint = rope_tabs
    sh, sh_spec = _ada_operand(ada, 1, 0, False)
    sc, sc_spec = _ada_operand(ada, 1, 1, False)
    wb = w.astype(BF16)
    wkn = wb[:, HD:2 * HD]
    wt = wb[:, 0:3 * HD].T
    full = lambda a: pl.BlockSpec(a.shape, lambda bi, i: (0,) * a.ndim)
    tok = lambda w_: pl.BlockSpec((None, tm, w_), lambda bi, i: (bi, i, 0))
    chan = lambda c_: pl.BlockSpec((None, c_, tm), lambda bi, i: (bi, 0, i))
    in_specs = [tok(d), sh_spec, sc_spec,
                pl.BlockSpec((tm, LANES), lambda bi, i: (i, 0)), pl.BlockSpec((tm, LANES), lambda bi, i: (i, 0)),
                pl.BlockSpec((HALF, tm), lambda bi, i: (0, i)), pl.BlockSpec((HALF, tm), lambda bi, i: (0, i)),
                full(wkn), full(wt)]
    args = [x, sh, sc, cosn, sinn, cost, sint, wkn, wt]
    out_shape = [jax.ShapeDtypeStruct((b, HD, t), BF16), jax.ShapeDtypeStruct((b, t, HD), BF16),
                 jax.ShapeDtypeStruct((b, 2 * HD, t), F32)]
    out_specs = [chan(HD), tok(HD), chan(2 * HD)]
    scratch = []
    if extra == "moba":
        out_shape.append(jax.ShapeDtypeStruct((b, t // tm, SUBLANES, HD), F32))
        out_specs.append(pl.BlockSpec((None, None, SUBLANES, HD), lambda bi, i: (bi, i, 0, 0)))
    elif extra == "dsa":
        wet = wb[:, 3 * HD:].T
        wkin = jnp.pad(wb[:, 3 * HD + DSA_NQI:3 * HD + DSA_NQI + DSA_IDX_DIM], ((0, 0), (0, LANES - DSA_IDX_DIM)))
        in_specs += [full(wet), full(wkin)]
        args += [wet, wkin]
        out_shape += [jax.ShapeDtypeStruct((b, DSA_NQI, t), BF16), jax.ShapeDtypeStruct((b, t, LANES), BF16),
                      jax.ShapeDtypeStruct((b, DSA_IDX_DIM, t), F32), jax.ShapeDtypeStruct((b, DSA_IDX_HEADS, t), F32)]
        out_specs += [chan(DSA_NQI), tok(LANES), chan(DSA_IDX_DIM), chan(DSA_IDX_HEADS)]
    else:
        wft = wb[:, 3 * HD:].T
        wfn = jnp.pad(wb[:, 3 * HD:], ((0, 0), (0, LANES - N_HEADS)))
        bft = jnp.broadcast_to(b_f.reshape(N_HEADS, 1), (N_HEADS, tm))
        bfn = jnp.pad(b_f, (0, LANES - N_HEADS)).reshape(1, LANES)
        in_specs += [full(wft), full(wfn), full(bft), full(bfn)]
        args += [wft, wfn, bft, bfn]
        out_shape += [jax.ShapeDtypeStruct((b, N_HEADS, t), F32), jax.ShapeDtypeStruct((b, N_HEADS, t), F32),
                      jax.ShapeDtypeStruct((b, t, LANES), F32)]
        out_specs += [chan(N_HEADS), chan(N_HEADS), tok(LANES)]
        scratch = [pltpu.VMEM((N_HEADS, LANES), F32), pltpu.VMEM((1, LANES), F32)]
    return pl.pallas_call(
        functools.partial(_projt_body, rope=rope, extra=extra),
        out_shape=out_shape,
        grid=(b, t // tm),
        in_specs=in_specs,
        out_specs=out_specs,
        scratch_shapes=scratch,
        compiler_params=_cp("arbitrary", "arbitrary"),
        name="projt_" + extra,
    )(*args)


def _oproj_body(*refs, n_grp):
    x_ref, gt_ref = refs[:2]
    o_refs = refs[2:2 + n_grp]
    l_refs = refs[2 + n_grp:2 + 2 * n_grp] if n_grp > 1 else ()
    w_ref, lg_ref, lb_ref, out_ref = refs[2 + len(o_refs) + len(l_refs):]
    if n_grp == 1:
        a = o_refs[0][...].astype(BF16)
    else:
        ls = [r[...] for r in l_refs]
        m = functools.reduce(jnp.maximum, ls)
        es = [jnp.exp(l - m) for l in ls]
        den = functools.reduce(lambda p, q: p + q, es)
        a = functools.reduce(lambda p, q: p + q, [(e / den) * r[...] for e, r in zip(es, o_refs)])
        a = a.astype(BF16)
    y = _dot(a, w_ref[...])
    out_ref[...] = _post_ln(x_ref[...], y, gt_ref[...], lg_ref[...], lb_ref[...], 1.0)


def _oproj(x, ada, outs, lses, w_o, ln_g, ln_b, per_row):
    b, t, d = x.shape
    tm = min(t, 512)
    gt, gt_spec = _ada_operand(ada, 1, 2, per_row)
    row = pl.BlockSpec((None, tm, d), lambda bi, i: (bi, i, 0))
    one = pl.BlockSpec((1, d), lambda bi, i: (0, 0))
    n_grp = len(outs)
    return pl.pallas_call(
        functools.partial(_oproj_body, n_grp=n_grp),
        out_shape=jax.ShapeDtypeStruct((b, t, d), F32),
        grid=(b, t // tm),
        in_specs=[row, gt_spec] + [row] * (n_grp + len(lses))
        + [pl.BlockSpec((HD, d), lambda bi, i: (0, 0)), one, one],
        out_specs=row,
        compiler_params=_cp("arbitrary", "arbitrary"),
        name="oproj",
    )(x, gt, *outs, *lses, w_o, ln_g.reshape(1, d), ln_b.reshape(1, d))


TQ = 256
TKC = 512


def _top3_rows(g, blk):
    idx = []
    for _ in range(MOBA_TOPK):
        m = jnp.max(g, axis=0, keepdims=True)
        ix = jnp.min(jnp.where(g == m, blk, 1e9), axis=0, keepdims=True)
        idx.append(ix)
        g = jnp.where(blk == ix, -3e38, g)
    return idx


def _flasht_body(*refs, mode):
    if mode == "moba":
        qt_ref, k_ref, vt_ref, km_ref, o_ref, s_scr = refs
    elif mode == "dsa":
        qt_ref, k_ref, vt_ref, mask_ref, o_ref, s_scr = refs
    else:
        qt_ref, k_ref, vt_ref, fn_ref, ft_ref, o_ref, s_scr = refs
    hp = pl.program_id(1)
    j = pl.program_id(2)
    top = _iota((LANES, TQ), 0) < HEAD_DIM
    qt = qt_ref[...].astype(F32)
    qh = (jnp.where(top, qt, 0.0).astype(BF16), jnp.where(top, 0.0, qt).astype(BF16))
    nck = (j + 2) // 2
    qpos = j * TQ + _iota((TKC, TQ), 1)
    first_blk = _iota((TKC, TQ), 0) < TQ

    if mode == "moba":
        km = km_ref[...]
        blk = _iota((km.shape[0], TQ), 0).astype(F32)
        jf = j.astype(F32)
        sel_idx = []
        for h in range(2):
            qf = jnp.where(top, qt, 0.0) if h == 0 else jnp.where(top, 0.0, qt)
            g = _dot(km, qf, precision=HIGHEST)
            sel_idx.append(_top3_rows(jnp.where(blk < jf, g, NEG_INF), blk))
    elif mode == "fox":
        fq = [ft_ref[pl.ds(2 * hp + h, 1), :] for h in range(2)]
        lane_k = _iota((TKC, LANES), 1)

    def scores(c, m_run, last):
        start = pl.multiple_of(c * TKC, TKC)
        kb = k_ref[pl.ds(start, TKC), :]
        if mode == "fox":
            f_tile = fn_ref[pl.ds(start, TKC), :]
        elif mode == "dsa":
            keep = mask_ref[pl.ds(start, TKC), :].astype(F32) > 0.5
        if last:
            causal = start + _iota((TKC, TQ), 0) <= qpos
        m_new = []
        for h in range(2):
            s = _dot(kb, qh[h])
            if mode == "fox":
                fk = jnp.sum(jnp.where(lane_k == 2 * hp + h, f_tile, 0.0), axis=1, keepdims=True)
                s = (s + fq[h]) - fk
            elif mode == "dsa":
                s = jnp.where(keep, s, NEG_INF)
            else:
                i1, i2, i3 = sel_idx[h]
                hits = []
                for half in range(2):
                    nf = (2 * c + half).astype(F32)
                    hit = jnp.where((i1 == nf) | (i2 == nf) | (i3 == nf), 1.0, 0.0)
                    if last:
                        hit = jnp.maximum(hit, jnp.where(nf >= jf, 1.0, 0.0))
                    hits.append(hit)
                s = jnp.where(jnp.where(first_blk, hits[0], hits[1]) > 0.5, s, NEG_INF)
            if last:
                s = jnp.where(causal, s, NEG_INF)
            s_scr[h, pl.ds(start, TKC), :] = s
            m_new.append(jnp.maximum(m_run[h], jnp.max(s.reshape(TKC // SUBLANES, SUBLANES, TQ), axis=0)))
        return tuple(m_new)

    def paired(fn, n, init):
        def quad(i, c):
            for u in range(4):
                c = fn(4 * i + u, c)
            return c

        carry = lax.fori_loop(0, n // 4, quad, init)
        base = (n // 4) * 4
        carry = lax.cond(n % 4 >= 2, lambda c: fn(base + 1, fn(base, c)), lambda c: c, carry)
        return lax.cond(n % 2 == 1, lambda c: fn(n - 1, c), lambda c: c, carry)

    neg = jnp.full((SUBLANES, TQ), -jnp.inf, F32)
    m_run = paired(lambda c, m: scores(c, m, False), nck - 1, (neg, neg))
    m_run = scores(nck - 1, m_run, True)
    m_fin = [jnp.max(m, axis=0, keepdims=True) for m in m_run]
    topk = _iota((LANES, TKC), 0) < HEAD_DIM

    def values(c, acc):
        start = pl.multiple_of(c * TKC, TKC)
        vb = vt_ref[:, pl.ds(start, TKC)]
        out = []
        for h in range(2):
            p = jnp.exp(s_scr[h, pl.ds(start, TKC), :] - m_fin[h]).astype(BF16)
            vext = (jnp.where(topk, vb, 1.0) if h == 0 else jnp.where(topk, 1.0, vb)).astype(BF16)
            out.append(acc[h] + _dot(vext, p))
        return tuple(out)

    acc0 = jnp.zeros((LANES, TQ), F32)
    acc = paired(values, nck, (acc0, acc0))
    l0 = acc[0][HEAD_DIM:HEAD_DIM + 1, :]
    l1 = acc[1][0:1, :]
    ot = jnp.where(top, acc[0] / l0, acc[1] / l1)
    o_ref[...] = ot.T.astype(BF16)


def _flasht(qt, kn, kvt, mode, extra=()):
    b, _, t = qt.shape
    assert t % TKC == 0, "the key chunks of the last query tile must stay inside the sequence"
    nq = t // TQ
    nhp = HD // LANES
    in_specs = [pl.BlockSpec((None, LANES, TQ), lambda bi, hp, j: (bi, hp, j)),
                pl.BlockSpec((None, t, LANES), lambda bi, hp, j: (bi, 0, hp)),
                pl.BlockSpec((None, LANES, t), lambda bi, hp, j: (bi, nhp + hp, 0))]
    if mode == "moba":
        nb = extra[0].shape[1]
        in_specs.append(pl.BlockSpec((None, nb, LANES), lambda bi, hp, j: (bi, 0, hp)))
    elif mode == "dsa":
        in_specs.append(pl.BlockSpec((None, t, TQ), lambda bi, hp, j: (bi, 0, j)))
    else:
        in_specs += [pl.BlockSpec((None, t, LANES), lambda bi, hp, j: (bi, 0, 0)),
                     pl.BlockSpec((None, N_HEADS, TQ), lambda bi, hp, j: (bi, 0, j))]
    return pl.pallas_call(
        functools.partial(_flasht_body, mode=mode),
        out_shape=jax.ShapeDtypeStruct((b, t, HD), BF16),
        grid=(b, nhp, nq),
        in_specs=in_specs,
        out_specs=pl.BlockSpec((None, TQ, LANES), lambda bi, hp, j: (bi, j, hp)),
        scratch_shapes=[pltpu.VMEM((2, t, TQ), F32)],
        compiler_params=_cp("arbitrary", "arbitrary", "arbitrary"),
        name="flash_" + mode,
    )(qt, kn, kvt, *extra)


ICH = 512


def _counter(key_scr, n_chunks, key_axis):
    nq = key_scr.shape[1 - key_axis]

    def count(pred):
        def body(c, cnt):
            st = pl.multiple_of(c * ICH, ICH)
            if key_axis == 1:
                kc = key_scr[:, pl.ds(st, ICH)]
                x = jnp.where(pred(kc, st + _iota(kc.shape, 1)), 1.0, 0.0)
                for s in range(ICH // LANES):
                    cnt = cnt + x[:, s * LANES:(s + 1) * LANES]
                return cnt
            kc = key_scr[pl.ds(st, ICH), :]
            x = jnp.where(pred(kc, st + _iota(kc.shape, 0)), 1.0, 0.0)
            return cnt + jnp.sum(x.reshape(ICH // SUBLANES, SUBLANES, nq), axis=0)

        if key_axis == 1:
            cnt = lax.fori_loop(0, n_chunks, body, jnp.zeros((nq, LANES), F32))
            return jnp.sum(cnt, axis=1, keepdims=True)
        cnt = lax.fori_loop(0, n_chunks, body, jnp.zeros((SUBLANES, nq), F32))
        return jnp.sum(cnt, axis=0, keepdims=True)

    return count


def _kth_largest_key(count, qshape, k, extra_key=None):
    def count_ge(cand):
        c = count(lambda kc, idx: kc >= cand)
        if extra_key is not None:
            c = c + jnp.where(extra_key >= cand, 1.0, 0.0)
        return c

    t0 = jnp.where(count_ge(jnp.zeros(qshape, I32)) >= k, 0, INT_MIN).astype(I32)

    def bit_body(it, cur):
        cand = cur + lax.shift_left(jnp.int32(1), 30 - it)
        return jnp.where(count_ge(cand) >= k, cand, cur)

    return lax.fori_loop(0, 31, bit_body, t0)


def _tie_limit(count, qshape, thr, need, n_bits):
    def bit_body(it, cur):
        cand = cur + lax.shift_left(jnp.int32(1), n_bits - 1 - it)
        c = count(lambda kc, idx: (kc == thr) & (idx < cand))
        return jnp.where(c < need, cand, cur)

    return lax.fori_loop(0, n_bits, bit_body, jnp.zeros(qshape, I32))


TQI = 256


def _dsa_index_body(qit_ref, wit_ref, kin_ref, mask_ref, key_scr, lim_scr, *, n_keep):
    i = pl.program_id(1)
    t = kin_ref.shape[0]
    nch = t // ICH
    qpos = i * TQI + _iota((1, TQI), 1)
    nact = ((i + 1) * TQI + ICH - 1) // ICH
    wi = wit_ref[...]
    zpad = jnp.zeros((LANES - DSA_IDX_DIM, TQI), BF16)
    qis = [jnp.concatenate([qit_ref[h * DSA_IDX_DIM:(h + 1) * DSA_IDX_DIM, :], zpad], axis=0)
           for h in range(DSA_IDX_HEADS)]

    for c in range(nch):
        @pl.when(c < nact)
        def _(c=c):
            kc = kin_ref[c * ICH:(c + 1) * ICH, :]
            acc = jnp.zeros((ICH, TQI), F32)
            for h in range(DSA_IDX_HEADS):
                acc = acc + wi[h:h + 1, :] * jnp.maximum(_dot(kc, qis[h]), 0.0)
            kpos = c * ICH + _iota((ICH, TQI), 0)
            key_scr[c * ICH:(c + 1) * ICH, :] = _sort_key(jnp.where(kpos <= qpos, acc, NEG_INF))

    kf = float(n_keep)
    count = _counter(key_scr, nact, 0)
    thr = _kth_largest_key(count, (1, TQI), kf)
    n_gt = count(lambda kc, idx: kc > thr)
    n_eq = count(lambda kc, idx: kc == thr)
    need = kf - n_gt
    lim_scr[...] = jnp.full(lim_scr.shape, t, I32)
    has_tie = jnp.max(jnp.where((n_eq > need) & (thr > NEG_KEY), 1.0, 0.0)) > 0.5

    @pl.when(has_tie)
    def _():
        lim_scr[...] = jnp.broadcast_to(_tie_limit(count, (1, TQI), thr, need, max(1, (t - 1).bit_length())),
                                        lim_scr.shape)

    lim = lim_scr[0:1, :]
    for c in range(nch):
        @pl.when(c < nact)
        def _(c=c):
            kc = key_scr[c * ICH:(c + 1) * ICH, :]
            kpos = c * ICH + _iota((ICH, TQI), 0)
            tie_ok = jnp.where(kc == thr, jnp.where(kpos <= lim, 1, 0), 0)
            sel = jnp.where(kc > thr, 1, tie_ok)
            mask_ref[c * ICH:(c + 1) * ICH, :] = jnp.where(kpos <= qpos, sel, 0).astype(jnp.int8)

        @pl.when(c >= nact)
        def _(c=c):
            mask_ref[c * ICH:(c + 1) * ICH, :] = jnp.zeros((ICH, TQI), jnp.int8)


def _dsa_index(qit, wit, kin, n_keep):
    b, _, t = qit.shape
    return pl.pallas_call(
        functools.partial(_dsa_index_body, n_keep=n_keep),
        out_shape=jax.ShapeDtypeStruct((b, t, t), jnp.int8),
        grid=(b, t // TQI),
        in_specs=[pl.BlockSpec((None, DSA_NQI, TQI), lambda bi, i: (bi, 0, i)),
                  pl.BlockSpec((None, DSA_IDX_HEADS, TQI), lambda bi, i: (bi, 0, i)),
                  pl.BlockSpec((None, t, LANES), lambda bi, i: (bi, 0, 0))],
        out_specs=pl.BlockSpec((None, t, TQI), lambda bi, i: (bi, 0, i)),
        scratch_shapes=[pltpu.VMEM((t, TQI), I32), pltpu.VMEM((SUBLANES, TQI), I32)],
        compiler_params=_cp("arbitrary", "arbitrary"),
        name="dsa_index",
    )(qit, wit, kin)


def _dil_body(q_ref, kp_ref, kc_ref, vp_ref, vc_ref, o_ref, lse_ref):
    i = pl.program_id(2)
    n = DIL_BLOCK
    lane = _iota((n, LANES), 1)
    lo = lane < HEAD_DIM
    row = _iota((n, n), 0)
    col = _iota((n, n), 1)
    ok_cur = row >= col
    ok_prev = (col >= row) & (i > 0)
    for hp in range(HD // LANES):
        sl = slice(hp * LANES, (hp + 1) * LANES)
        q = q_ref[:, sl].astype(F32)
        kc = kc_ref[:, sl].astype(BF16)
        kp = kp_ref[:, sl].astype(BF16)
        vc = vc_ref[:, sl].astype(BF16)
        vp = vp_ref[:, sl].astype(BF16)
        outs, lses = [], []
        for h in range(2):
            qh = (jnp.where(lo, q, 0.0) if h == 0 else jnp.where(lo, 0.0, q)).astype(BF16)
            sc = jnp.where(ok_cur, _dot_nt(qh, kc), NEG_INF)
            sp = jnp.where(ok_prev, _dot_nt(qh, kp), NEG_INF)
            m = jnp.maximum(jnp.max(sc, axis=1, keepdims=True), jnp.max(sp, axis=1, keepdims=True))
            pc = jnp.exp(sc - m)
            pp = jnp.exp(sp - m)
            l = jnp.sum(pc, axis=1, keepdims=True) + jnp.sum(pp, axis=1, keepdims=True)
            outs.append((_dot(pc.astype(BF16), vc) + _dot(pp.astype(BF16), vp)) / l)
            lses.append(m + jnp.log(l))
        o_ref[:, sl] = jnp.where(lo, outs[0], outs[1])
        lse_ref[:, sl] = jnp.where(lo, lses[0], lses[1])


def _dil_band(q, kv, dil):
    b, t, _ = q.shape
    ls = t // dil
    nbk = ls // DIL_BLOCK
    qv = q.reshape(b, ls, dil * HD)
    kvv = kv.reshape(b, ls, dil * 2 * HD)
    blk = lambda f: pl.BlockSpec((None, DIL_BLOCK, HD), f)
    cur = lambda off: (lambda bi, r, i: (bi, i, 2 * r + off))
    prev = lambda off: (lambda bi, r, i: (bi, jnp.maximum(i - 1, 0), 2 * r + off))
    o, lse = pl.pallas_call(
        _dil_body,
        out_shape=[jax.ShapeDtypeStruct((b, ls, dil * HD), F32)] * 2,
        grid=(b, dil, nbk),
        in_specs=[blk(lambda bi, r, i: (bi, i, r)), blk(prev(0)), blk(cur(0)), blk(prev(1)), blk(cur(1))],
        out_specs=[blk(lambda bi, r, i: (bi, i, r))] * 2,
        compiler_params=_cp("arbitrary", "arbitrary", "arbitrary"),
        name="dil_band",
    )(qv, kvv, kvv, kvv, kvv)
    return o.reshape(b, t, HD), lse.reshape(b, t, HD)


APG = 8


def _heads3(x):
    return x.reshape(x.shape[0] // HEAD_DIM, HEAD_DIM, x.shape[1])


def _paged_attn_body(pt_ref, q_ref, *refs):
    pages = refs[:APG]
    bias_ref, kvn_ref, bn_ref, o_ref, m_scr, l_scr, acc_scr = refs[APG:]
    s_idx = pl.program_id(1)
    q3 = _heads3(q_ref[...])
    lane0 = _iota((N_HEADS, LANES), 1) == 0

    @pl.when(s_idx == 0)
    def _():
        s_new = jnp.sum(q3 * _heads3(kvn_ref[0:HD, :]), axis=1) + bn_ref[...]
        m_scr[...] = s_new
        l_scr[...] = jnp.where(lane0, 1.0, 0.0)
        acc_scr[...] = jnp.where(_iota((HD, LANES), 1) == 0, kvn_ref[HD:2 * HD, :], 0.0)

    bh = bias_ref.shape[0]
    for h in range(N_HEADS):
        rk = slice(h * HEAD_DIM, (h + 1) * HEAD_DIM)
        rv = slice(HD + h * HEAD_DIM, HD + (h + 1) * HEAD_DIM)
        hb = h if bh > 1 else 0
        qh = q_ref[rk, :]
        m_old = m_scr[h:h + 1, :]
        ss = [jnp.sum(pages[p][rk, :] * qh, axis=0, keepdims=True)
              + bias_ref[hb:hb + 1, p * PAGE_SIZE:(p + 1) * PAGE_SIZE] for p in range(APG)]
        mn = jnp.maximum(m_old, jnp.max(functools.reduce(jnp.maximum, ss), axis=1, keepdims=True))
        alpha = jnp.exp(m_old - mn)
        acc = alpha * acc_scr[rk, :]
        l = alpha * l_scr[h:h + 1, :]
        for p in range(APG):
            pr = jnp.exp(ss[p] - mn)
            l = l + pr
            acc = acc + pr * pages[p][rv, :]
        m_scr[h:h + 1, :] = mn
        l_scr[h:h + 1, :] = l
        acc_scr[rk, :] = acc

    @pl.when(s_idx == pl.num_programs(1) - 1)
    def _():
        lsum = jnp.sum(l_scr[...], axis=1, keepdims=True)
        o3 = jnp.sum(_heads3(acc_scr[...]), axis=2, keepdims=True) / lsum[:, :, None]
        o_ref[...] = jnp.broadcast_to(o3, (N_HEADS, HEAD_DIM, LANES)).reshape(HD, LANES)


def _paged_attn(page_table, q_rep, cache_t, bias, kvn_rep, bias_new):
    bs, n_pages = page_table.shape
    bh = bias.shape[1]
    page_spec = lambda p: pl.BlockSpec(
        (None, 2 * HD, PAGE_SIZE), lambda bi, s, pt: (pt[bi * n_pages + s * APG + p], 0, 0))
    grid_spec = pltpu.PrefetchScalarGridSpec(
        num_scalar_prefetch=1,
        grid=(bs, n_pages // APG),
        in_specs=[pl.BlockSpec((None, HD, LANES), lambda bi, s, pt: (bi, 0, 0))]
        + [page_spec(p) for p in range(APG)]
        + [pl.BlockSpec((None, bh, APG * PAGE_SIZE), lambda bi, s, pt: (bi, 0, s)),
           pl.BlockSpec((None, 2 * HD, LANES), lambda bi, s, pt: (bi, 0, 0)),
           pl.BlockSpec((None, bh, LANES), lambda bi, s, pt: (bi, 0, 0))],
        out_specs=pl.BlockSpec((None, HD, LANES), lambda bi, s, pt: (bi, 0, 0)),
        scratch_shapes=[pltpu.VMEM((N_HEADS, LANES), F32), pltpu.VMEM((N_HEADS, LANES), F32),
                        pltpu.VMEM((HD, LANES), F32)],
    )
    return pl.pallas_call(
        _paged_attn_body,
        out_shape=jax.ShapeDtypeStruct((bs, HD, LANES), F32),
        grid_spec=grid_spec,
        compiler_params=_cp("arbitrary", "arbitrary"),
        name="paged_attn",
    )(page_table.reshape(-1), q_rep, *([cache_t] * APG), bias, kvn_rep, bias_new)


def _top3_lanes(g, blk):
    idx = []
    for _ in range(MOBA_TOPK):
        m = jnp.max(g, axis=1, keepdims=True)
        ix = jnp.min(jnp.where(g == m, blk, 1e9), axis=1, keepdims=True)
        idx.append(ix)
        g = jnp.where(blk == ix, -3e38, g)
    return idx


GBS = 8
GPB = MOBA_BLOCK // PAGE_SIZE


def _moba_gate_body(pt_ref, q_ref, *refs):
    pages = refs[:GBS * GPB]
    bias_ref, gate_scr = refs[GBS * GPB:]
    n = pl.program_id(1)
    nb = pl.num_programs(1)
    past = bias_ref.shape[1]
    lane = _iota((N_HEADS, LANES), 1)
    q3 = _heads3(q_ref[...])

    @pl.when(n == 0)
    def _():
        gate_scr[...] = jnp.full(gate_scr.shape, NEG_INF, F32)

    gates = gate_scr[...]
    for k in range(GBS):
        ksum = functools.reduce(lambda a, b: a + b, [pages[k * GPB + p][...] for p in range(GPB)])
        g = jnp.sum(jnp.sum(_heads3(ksum) * q3, axis=1), axis=1, keepdims=True) * (1.0 / MOBA_BLOCK)
        gates = jnp.where(lane == n * GBS + k, g, gates)
    gate_scr[...] = gates

    @pl.when(n == nb - 1)
    def _():
        i1, i2, i3 = _top3_lanes(gate_scr[...], lane.astype(F32))
        blk = (_iota((N_HEADS, past), 1) // MOBA_BLOCK).astype(F32)
        bias_ref[...] = jnp.where((blk == i1) | (blk == i2) | (blk == i3), 0.0, NEG_INF)


def _moba_gate(page_table, q_rep, cache_t):
    bs, n_pages = page_table.shape
    pps = GBS * GPB
    past = n_pages * PAGE_SIZE
    page_spec = lambda p: pl.BlockSpec(
        (None, HD, PAGE_SIZE), lambda bi, n, pt: (pt[bi * n_pages + n * pps + p], 0, 0))
    grid_spec = pltpu.PrefetchScalarGridSpec(
        num_scalar_prefetch=1,
        grid=(bs, n_pages // pps),
        in_specs=[pl.BlockSpec((None, HD, LANES), lambda bi, n, pt: (bi, 0, 0))] + [page_spec(p) for p in range(pps)],
        out_specs=pl.BlockSpec((None, N_HEADS, past), lambda bi, n, pt: (bi, 0, 0)),
        scratch_shapes=[pltpu.VMEM((N_HEADS, LANES), F32)],
    )
    return pl.pallas_call(
        _moba_gate_body,
        out_shape=jax.ShapeDtypeStruct((bs, N_HEADS, past), F32),
        grid_spec=grid_spec,
        compiler_params=_cp("arbitrary", "arbitrary"),
        name="moba_gate",
    )(page_table.reshape(-1), q_rep, *([cache_t] * pps))


SPG = 8


def _dsa_score_body(pt_ref, qi_ref, wi_ref, *refs):
    pages = refs[:SPG]
    out_ref = refs[SPG]
    qi = qi_ref[...]
    wi = wi_ref[...]
    for p in range(SPG):
        d = _dot(qi, pages[p][...].astype(BF16))
        out_ref[:, p * PAGE_SIZE:(p + 1) * PAGE_SIZE] = jnp.sum(wi * jnp.maximum(d, 0.0), axis=0, keepdims=True)


def _dsa_scores(page_table, qi, wi, kidx_t):
    bs, n_pages = page_table.shape
    page_spec = lambda p: pl.BlockSpec(
        (None, DSA_IDX_DIM, PAGE_SIZE), lambda bi, s, pt: (pt[bi * n_pages + s * SPG + p], 0, 0))
    grid_spec = pltpu.PrefetchScalarGridSpec(
        num_scalar_prefetch=1,
        grid=(bs, n_pages // SPG),
        in_specs=[pl.BlockSpec((None, DSA_IDX_HEADS, DSA_IDX_DIM), lambda bi, s, pt: (bi, 0, 0)),
                  pl.BlockSpec((None, DSA_IDX_HEADS, 1), lambda bi, s, pt: (bi, 0, 0))]
        + [page_spec(p) for p in range(SPG)],
        out_specs=pl.BlockSpec((None, 1, SPG * PAGE_SIZE), lambda bi, s, pt: (bi, 0, s)),
    )
    return pl.pallas_call(
        _dsa_score_body,
        out_shape=jax.ShapeDtypeStruct((bs, 1, n_pages * PAGE_SIZE), F32),
        grid_spec=grid_spec,
        compiler_params=_cp("arbitrary", "arbitrary"),
        name="dsa_scores",
    )(page_table.reshape(-1), qi, wi, *([kidx_t] * SPG))


def _dsa_select_body(sc_ref, qi_ref, kiwi_ref, bias_ref, bn_ref, key_scr, lim_scr, *, n_keep):
    bs, past = sc_ref.shape
    nch = past // ICH
    for c in range(nch):
        key_scr[:, c * ICH:(c + 1) * ICH] = _sort_key(sc_ref[:, c * ICH:(c + 1) * ICH])
    kiwi = kiwi_ref[...]
    qi = qi_ref[...].astype(F32)
    s_new = jnp.zeros((bs, 1), F32)
    for h in range(DSA_IDX_HEADS):
        d = jnp.sum(qi[:, h * DSA_IDX_DIM:(h + 1) * DSA_IDX_DIM] * kiwi[:, 0:DSA_IDX_DIM],
                    axis=1, keepdims=True)
        s_new = s_new + kiwi[:, DSA_IDX_DIM + h:DSA_IDX_DIM + h + 1] * jnp.maximum(d, 0.0)
    key_new = _sort_key(s_new)

    kf = float(n_keep)
    count = _counter(key_scr, nch, 1)
    thr = _kth_largest_key(count, (bs, 1), kf, extra_key=key_new)
    n_gt = count(lambda kc, idx: kc > thr) + jnp.where(key_new > thr, 1.0, 0.0)
    n_eq = count(lambda kc, idx: kc == thr)
    need = kf - n_gt
    lim_scr[...] = jnp.full(lim_scr.shape, past, I32)
    has_tie = jnp.max(jnp.where(n_eq > need, 1.0, 0.0)) > 0.5

    @pl.when(has_tie)
    def _():
        lim_scr[...] = jnp.broadcast_to(_tie_limit(count, (bs, 1), thr, need, past.bit_length()), lim_scr.shape)

    lim = lim_scr[:, 0:1]
    for c in range(nch):
        kc = key_scr[:, c * ICH:(c + 1) * ICH]
        kpos = c * ICH + _iota((bs, ICH), 1)
        tie_ok = jnp.where(kc == thr, jnp.where(kpos <= lim, 0.0, NEG_INF), NEG_INF)
        bias_ref[:, c * ICH:(c + 1) * ICH] = jnp.where(kc > thr, 0.0, tie_ok)
    new_ok = (key_new > thr) | ((key_new == thr) & (n_eq < need))
    bn_ref[...] = jnp.broadcast_to(jnp.where(new_ok, 0.0, NEG_INF), bn_ref.shape)


def _dsa_select(scores, qi, kiwi, n_keep):
    bs, past = scores.shape
    return pl.pallas_call(
        functools.partial(_dsa_select_body, n_keep=n_keep),
        out_shape=[jax.ShapeDtypeStruct((bs, past), F32), jax.ShapeDtypeStruct((bs, LANES), F32)],
        scratch_shapes=[pltpu.VMEM((bs, past), I32), pltpu.VMEM((bs, LANES), I32)],
        compiler_params=pltpu.CompilerParams(vmem_limit_bytes=VMEM_LIMIT),
        name="dsa_select",
    )(scores, qi, kiwi)


FPG = 8


def _fox_bias_body(pt_ref, lfn_ref, *refs):
    pages = refs[:FPG]
    bias_ref, carry_scr = refs[FPG:]

    @pl.when(pl.program_id(1) == 0)
    def _():
        carry_scr[...] = lfn_ref[...]

    later = (_iota((PAGE_SIZE, PAGE_SIZE), 0) > _iota((PAGE_SIZE, PAGE_SIZE), 1)).astype(F32)
    carry = carry_scr[...]
    for p in range(FPG - 1, -1, -1):
        x = pages[p][...]
        bias_ref[:, p * PAGE_SIZE:(p + 1) * PAGE_SIZE] = _dot(x, later, precision=HIGHEST) + carry
        carry = carry + jnp.sum(x, axis=1, keepdims=True)
    carry_scr[...] = carry


def _fox_bias(page_table, logf_t, lf_new):
    bs, n_pages = page_table.shape
    nst = n_pages // FPG
    page_spec = lambda p: pl.BlockSpec(
        (None, N_HEADS, PAGE_SIZE),
        lambda bi, s, pt: (pt[bi * n_pages + (nst - 1 - s) * FPG + p], 0, 0))
    grid_spec = pltpu.PrefetchScalarGridSpec(
        num_scalar_prefetch=1,
        grid=(bs, nst),
        in_specs=[pl.BlockSpec((None, N_HEADS, LANES), lambda bi, s, pt: (bi, 0, 0))]
        + [page_spec(p) for p in range(FPG)],
        out_specs=pl.BlockSpec((None, N_HEADS, FPG * PAGE_SIZE), lambda bi, s, pt: (bi, 0, nst - 1 - s)),
        scratch_shapes=[pltpu.VMEM((N_HEADS, LANES), F32)],
    )
    return pl.pallas_call(
        _fox_bias_body,
        out_shape=jax.ShapeDtypeStruct((bs, N_HEADS, n_pages * PAGE_SIZE), F32),
        grid_spec=grid_spec,
        compiler_params=_cp("arbitrary", "arbitrary"),
        name="fox_bias",
    )(page_table.reshape(-1), lf_new, *([logf_t] * FPG))


def _dil_sample_body(*refs):
    n_grp = len(DIL_WINDOWS)
    q_refs = refs[:n_grp]
    kvn_refs = refs[n_grp:2 * n_grp]
    buf_refs = refs[2 * n_grp:3 * n_grp]
    y_ref = refs[3 * n_grp]
    win_refs = refs[3 * n_grp + 1:]
    outs, lses = [], []
    for g in range(n_grp):
        dil = DIL_DILATIONS[g]
        w = buf_refs[g].shape[2]
        q3 = _heads3(q_refs[g][...])
        kn = kvn_refs[g][0]
        vn = kvn_refs[g][1]
        kb = buf_refs[g][0]
        vb = buf_refs[g][1]
        nt = w // LANES
        s_new = jnp.sum(q3 * _heads3(kn), axis=1)[:, 0:1]
        ss = [jnp.sum(_heads3(kb[:, c * LANES:(c + 1) * LANES]) * q3, axis=1) for c in range(nt)]
        lane = _iota((2, LANES), 1)
        ss = [jnp.where(((c * LANES + lane) % dil) == 0, s, NEG_INF) for c, s in enumerate(ss)]
        m = functools.reduce(jnp.maximum, [jnp.max(s, axis=1, keepdims=True) for s in ss] + [s_new])
        ps = [jnp.exp(s - m) for s in ss]
        p_new = jnp.exp(s_new - m)
        l = functools.reduce(lambda a, b: a + b, [jnp.sum(p, axis=1, keepdims=True) for p in ps]) + p_new
        acc = functools.reduce(lambda a, b: a + b,
                               [p[:, None, :] * _heads3(vb[:, c * LANES:(c + 1) * LANES]) for c, p in enumerate(ps)])
        o = jnp.sum(acc, axis=2, keepdims=True) + p_new[:, :, None] * _heads3(vn)[:, :, 0:1]
        outs.append(o / l[:, :, None])
        lses.append(m + jnp.log(l))
        last = _iota((LANES, w), 1) == w - 1
        win_refs[g][0] = jnp.where(last, kn[:, 0:1], pltpu.roll(kb, w - 1, 1))
        win_refs[g][1] = jnp.where(last, vn[:, 0:1], pltpu.roll(vb, w - 1, 1))
    m = functools.reduce(jnp.maximum, lses)
    es = [jnp.exp(l - m) for l in lses]
    den = functools.reduce(lambda a, b: a + b, es)
    y = functools.reduce(lambda a, b: a + b, [(e / den)[:, :, None] * o for e, o in zip(es, outs)])
    y_ref[...] = jnp.broadcast_to(y, (2, HEAD_DIM, LANES)).reshape(LANES, LANES)


def _dil_sample(q_reps, kvn_reps, bufs_t):
    bs = q_reps[0].shape[0]
    nhp = HD // LANES
    args, specs = [], []
    for q in q_reps:
        args.append(q)
        specs.append(pl.BlockSpec((None, LANES, LANES), lambda bi, hp: (bi, hp, 0)))
    for kvn in kvn_reps:
        args.append(kvn)
        specs.append(pl.BlockSpec((None, 2, LANES, LANES), lambda bi, hp: (bi, 0, hp, 0)))
    win_specs, win_shapes = [], []
    for buf in bufs_t:
        w = buf.shape[3]
        args.append(buf)
        spec = pl.BlockSpec((None, 2, LANES, w), lambda bi, hp: (bi, 0, hp, 0))
        specs.append(spec)
        win_specs.append(spec)
        win_shapes.append(jax.ShapeDtypeStruct(buf.shape, F32))
    return pl.pallas_call(
        _dil_sample_body,
        out_shape=[jax.ShapeDtypeStruct((bs, HD, LANES), F32)] + win_shapes,
        grid=(bs, nhp),
        in_specs=specs,
        out_specs=[pl.BlockSpec((None, LANES, LANES), lambda bi, hp: (bi, hp, 0))] + win_specs,
        compiler_params=_cp("arbitrary", "arbitrary"),
        name="dil_sample",
    )(*args)


def _rope_angles(pos):
    inv = ROPE_THETA ** (-jnp.arange(HALF, dtype=F32) / HALF)
    ang = pos.astype(F32)[:, None] * inv
    return jnp.cos(ang), jnp.sin(ang)


def _rope_tables(pos):
    cos, sin = _rope_angles(pos)
    return (jnp.concatenate([cos, cos, cos, cos], -1), jnp.concatenate([-sin, sin, -sin, sin], -1))


def _pad_cols(w, n):
    return jnp.pad(w, ((0, 0), (0, n - w.shape[1])))


def _kv5(kv):
    return kv.reshape(kv.shape[0], kv.shape[1], 2, N_HEADS, HEAD_DIM)


def _kv5_t(kvt):
    b, _, t = kvt.shape
    return kvt.reshape(b, 2, N_HEADS, HEAD_DIM, t).transpose(0, 4, 1, 2, 3)


def _chan_major(x5):
    n, rows = x5.shape[:2]
    return x5.transpose(0, 2, 3, 4, 1).reshape(n, 2 * HD, rows)


def _lane_rep(x):
    return jnp.broadcast_to(x.astype(F32)[..., None], x.shape + (LANES,))


def kernel(x_prompt, x_sample, c_prompt, c_sample, cache_kv_moba, cache_kv_dsa, cache_kidx_dsa, cache_kv_fox, cache_logf_fox, state_win_d1, state_win_d4, state_win_d16, page_table, w_ada, b_ada, ln_g, ln_b, ffn_w_up, ffn_w_down, moba_w_qkv, moba_w_o, dsa_w_in, dsa_w_o, fox_w_in, fox_b_f, fox_w_o, dil_w_qkv, dil_w_o):
    bp, seq, d = x_prompt.shape
    bs = x_sample.shape[0]
    past = page_table.shape[1] * PAGE_SIZE
    depth = w_ada.shape[0]

    mp = -(-(bp + bs) // SUBLANES) * SUBLANES
    c_all = jnp.pad(jnp.concatenate([c_prompt, c_sample], 0), ((0, mp - bp - bs), (0, 0)))
    ada_all = _ada_all(c_all, w_ada, b_ada)
    ada_p = ada_all[:, :bp].reshape(depth, bp, N_ADA, 1, d)
    ada_s = ada_all[:, bp:bp + bs]

    pos_p = jnp.arange(seq, dtype=I32)
    cos_p, sin_p = _rope_tables(pos_p)
    cos_a, sin_a = _rope_angles(pos_p)
    tabs_p = (cos_p, sin_p, cos_a.T, sin_a.T)
    tabs_id = (jnp.ones((seq, LANES), F32), jnp.zeros((seq, LANES), F32),
               jnp.ones((HALF, seq), F32), jnp.zeros((HALF, seq), F32))
    cos_s, sin_s = _rope_tables(jnp.full((1,), past, I32))
    one_c, zero_s = jnp.ones((1, LANES), F32), jnp.zeros((1, LANES), F32)

    w_up = ffn_w_up.astype(BF16)
    w_down = ffn_w_down.astype(BF16)
    w_moba = moba_w_qkv.astype(BF16)
    w_dsa = _pad_cols(dsa_w_in, 3 * HD + DSA_NQI + LANES).astype(BF16)
    w_fox = _pad_cols(fox_w_in, 3 * HD + LANES).astype(BF16)
    w_dil = dil_w_qkv.astype(BF16)
    b_fox = jnp.pad(fox_b_f, (0, LANES - N_HEADS)).reshape(1, LANES)
    w_outs = [w.astype(BF16) for w in (moba_w_o, dsa_w_o, fox_w_o, dil_w_o)]
    zero_bn = jnp.zeros((bs, N_HEADS, LANES), F32)

    xp = x_prompt
    xs = x_sample.reshape(1, bs, d)
    res = {}
    for i in range(depth):
        ap, as_ = ada_p[i], ada_s[i]
        xp = _ffn(xp, ap, 0, w_up[i, 0], w_down[i, 0], ln_g[i, 0], ln_b[i, 0], False)
        xs = _ffn(xs, as_, 0, w_up[i, 0], w_down[i, 0], ln_g[i, 0], ln_b[i, 0], True)
        kind = i % 4
        lp = ()
        if kind == 0:
            qt, kn, kvt, kmean8 = _projt(xp, ap, moba_w_qkv, tabs_p, True, "moba")
            kmean = kmean8[:, :, :PROJT_TM // MOBA_BLOCK].reshape(bp, seq // MOBA_BLOCK, HD)
            op = [_flasht(qt, kn, kvt, "moba", (kmean,))]
            qs, kvs = _proj(xs, as_, w_moba, cos_s, sin_s, True, True)
            cache_t = _chan_major(cache_kv_moba)
            q_rep = _lane_rep(qs[0])
            bias = _moba_gate(page_table, q_rep, cache_t)
            o_rep = _paged_attn(page_table, q_rep, cache_t, bias, _lane_rep(kvs[0]), zero_bn)
            res["kv_moba"] = (_kv5_t(kvt), kvs)
        elif kind == 1:
            qt, kn, kvt, qit, kin, kit, wit = _projt(xp, ap, dsa_w_in, tabs_p, True, "dsa")
            mask = _dsa_index(qit, wit, kin, min(DSA_TOPK_MAX, seq // 4))
            op = [_flasht(qt, kn, kvt, "dsa", (mask,))]
            qs, kvs, qis, kiwis = _proj(xs, as_, w_dsa, cos_s, sin_s, True, True, "dsa")
            wi = kiwis[0, :, DSA_IDX_DIM:DSA_IDX_DIM + DSA_IDX_HEADS].reshape(bs, DSA_IDX_HEADS, 1)
            scores = _dsa_scores(page_table, qis[0].reshape(bs, DSA_IDX_HEADS, DSA_IDX_DIM), wi,
                                 cache_kidx_dsa.transpose(0, 2, 1))
            bias, bias_new = _dsa_select(scores.reshape(bs, past), qis[0], kiwis[0],
                                         min(DSA_TOPK_MAX, (past + 1) // 4))
            o_rep = _paged_attn(page_table, _lane_rep(qs[0]), _chan_major(cache_kv_dsa), bias.reshape(bs, 1, past),
                                _lane_rep(kvs[0]), bias_new.reshape(bs, 1, LANES))
            res["kv_dsa"] = (_kv5_t(kvt), kvs)
            res["kidx_dsa"] = (kit.transpose(0, 2, 1), kiwis[..., :DSA_IDX_DIM].reshape(bs, 1, DSA_IDX_DIM))
        elif kind == 2:
            qt, kn, kvt, lft, ft, fn = _projt(xp, ap, fox_w_in, tabs_id, False, "fox", fox_b_f)
            op = [_flasht(qt, kn, kvt, "fox", (fn, ft))]
            qs, kvs, lfs = _proj(xs, as_, w_fox, one_c, zero_s, True, False, "fox", b_fox)
            lf_new = jnp.broadcast_to(lfs[0, :, :N_HEADS, None], (bs, N_HEADS, LANES))
            bias = _fox_bias(page_table, cache_logf_fox.transpose(0, 2, 1), lf_new)
            o_rep = _paged_attn(page_table, _lane_rep(qs[0]), _chan_major(cache_kv_fox), bias,
                                _lane_rep(kvs[0]), zero_bn)
            res["kv_fox"] = (_kv5_t(kvt), kvs)
            res["logf_fox"] = (lft.transpose(0, 2, 1), lfs[..., :N_HEADS].reshape(bs, 1, N_HEADS))
        else:
            op, lp, q_reps, kvn_reps, bufs_t = [], [], [], [], []
            bufs = (state_win_d1, state_win_d4, state_win_d16)
            for g, (win, dil) in enumerate(zip(DIL_WINDOWS, DIL_DILATIONS)):
                wg = w_dil[:, g * 3 * HD:(g + 1) * 3 * HD]
                qp, kvp = _proj(xp, ap, wg, cos_p, sin_p, False, True)
                o, lse = _dil_band(qp, kvp, dil)
                op.append(o)
                lp.append(lse)
                qs, kvs = _proj(xs, as_, wg, cos_s, sin_s, True, True)
                q_reps.append(_lane_rep(qs[0]))
                kvn_reps.append(_lane_rep(kvs[0]).reshape(bs, 2, HD, LANES))
                bufs_t.append(_chan_major(bufs[g]).reshape(bs, 2, HD, bufs[g].shape[1]))
                res["win_%d" % g] = [_kv5(kvp[:, seq - min(win, seq):])]
            o_rep, *wins = _dil_sample(q_reps, kvn_reps, bufs_t)
            for g, wn in enumerate(wins):
                res["win_%d" % g].append(_kv5_t(wn.reshape(bs, 2 * HD, wn.shape[3])))
        os_ = [o_rep[:, :, 0].reshape(1, bs, HD)]
        xp = _oproj(xp, ap, op, lp, w_outs[kind], ln_g[i, 1], ln_b[i, 1], False)
        xs = _oproj(xs, as_, os_, (), w_outs[kind], ln_g[i, 1], ln_b[i, 1], True)
        xp = _ffn(xp, ap, 2, w_up[i, 1], w_down[i, 1], ln_g[i, 2], ln_b[i, 2], False)
        xs = _ffn(xs, as_, 2, w_up[i, 1], w_down[i, 1], ln_g[i, 2], ln_b[i, 2], True)

    kv_out = lambda name: (res[name][0], _kv5(res[name][1].reshape(bs, 1, 2 * HD)))
    return (xp, xs.reshape(bs, 1, d),
            *kv_out("kv_moba"), *kv_out("kv_dsa"), *res["kidx_dsa"],
            *kv_out("kv_fox"), *res["logf_fox"],
            *res["win_0"], *res["win_1"], *res["win_2"])
```
